```python
import math
import jax, jax.numpy as jnp
from jax import lax
import numpy as np

D_MODEL = 4096
BATCH = 4
SEQ = 2048
DEPTH = 1
DEC_BATCH = 128
DEC_SEQ = 8
PAST_LEN = 16384
PAGE_SIZE = 128

M_HEADS = 4
M_V_DIM = (D_MODEL // 2) // M_HEADS
M_QK_DIM = M_V_DIM // 2
M_CHUNK = 64
GATE_CAP = 15.0
A_HEADS = 16
A_NOPE = 128
A_ROPE = 64
A_V = (D_MODEL // 2) // A_HEADS
Q_LORA = D_MODEL // 4
KV_LORA = 512
ROPE_THETA = 10000.0
Q_BLOCK = 128
N_GROUPS = 8
EXPERTS_PER_GROUP = 8
N_EXPERTS = N_GROUPS * EXPERTS_PER_GROUP
TOP_K = 2
D_EXPERT = D_MODEL // 4
MAX_BLOCK_ROWS = 128
MIN_BLOCK_ROWS = 8
PLE_DIM = 256
EPS = 1e-6
IN_COLS = 2 * M_HEADS * M_QK_DIM + 2 * M_HEADS * M_V_DIM + 2 * M_HEADS + Q_LORA + KV_LORA + A_ROPE

kernel_name = "hybrid_mlstm_mla_hiermoe_step"

F32 = jnp.float32


def rms_norm(x, g):
    xf = x.astype(F32)
    y = xf * lax.rsqrt(jnp.mean(xf * xf, axis=-1, keepdims=True) + EPS)
    return (y * g.astype(F32)).astype(x.dtype)


def soft_cap(x):
    return GATE_CAP * jnp.tanh(x / GATE_CAP)


def rope(x, pos):
    half = x.shape[-1] // 2
    inv = 1.0 / (ROPE_THETA ** (jnp.arange(half, dtype=F32) / half))
    ang = pos.astype(F32)[:, None] * inv[None, :]
    shape = (1, pos.shape[0]) + (1,) * (x.ndim - 3) + (half,)
    cos = jnp.cos(ang).reshape(shape)
    sin = jnp.sin(ang).reshape(shape)
    xf = x.astype(F32)
    x1, x2 = xf[..., :half], xf[..., half:]
    return jnp.concatenate([x1 * cos - x2 * sin, x2 * cos + x1 * sin], axis=-1).astype(x.dtype)


def split_cols(z):
    sizes = (M_HEADS * M_QK_DIM, M_HEADS * M_QK_DIM, M_HEADS * M_V_DIM, M_HEADS * M_V_DIM,
             M_HEADS, M_HEADS, Q_LORA, KV_LORA + A_ROPE)
    idx = np.cumsum(sizes)[:-1].tolist()
    return jnp.split(z, idx, axis=-1)


def mixer_inputs(x, pos, norm_mix, w_in, b_igate, b_fgate, q_a_norm, w_q_b, kv_a_norm):
    bx, t, _ = x.shape
    z = rms_norm(x, norm_mix) @ w_in
    mq, mk, mv, mo, mi, mf, aq, akv = split_cols(z)

    def heads(a, d):
        return a.reshape(bx, t, -1, d).transpose(0, 2, 1, 3).astype(F32)

    q = heads(mq, M_QK_DIM) * (M_QK_DIM ** -0.5)
    k = heads(mk, M_QK_DIM)
    v = heads(mv, M_V_DIM)
    i_log = soft_cap(mi.astype(F32) + b_igate.astype(F32)).transpose(0, 2, 1)
    f_log = jax.nn.log_sigmoid(soft_cap(mf.astype(F32) + b_fgate.astype(F32))).transpose(0, 2, 1)
    qa = (rms_norm(aq, q_a_norm) @ w_q_b).reshape(bx, t, A_HEADS, A_NOPE + A_ROPE)
    q_nope = qa[..., :A_NOPE]
    q_rope = rope(qa[..., A_NOPE:], pos)
    c = rms_norm(akv[..., :KV_LORA], kv_a_norm)
    k_r = rope(akv[..., KV_LORA:], pos)
    return (q, k, v, i_log, f_log), mo, (q_nope, q_rope, c, k_r)


def mlstm_chunkwise(q, k, v, i_log, f_log, c0, n0, m0):
    b, h, t, dk = q.shape
    dv = v.shape[-1]
    L = math.gcd(t, M_CHUNK)
    nc = t // L

    def to_chunks(a):
        return jnp.moveaxis(a.reshape(a.shape[:2] + (nc, L) + a.shape[3:]), 2, 0)

    causal = jnp.tril(jnp.ones((L, L), dtype=bool))

    def step(carry, xs):
        C, n, m = carry
        qc, kc, vc, ic, fc = xs
        bcum = jnp.cumsum(fc, axis=-1)
        dlog = jnp.where(causal, bcum[..., :, None] - bcum[..., None, :] + ic[..., None, :], -jnp.inf)
        inter = bcum + m[..., None]
        m_t = jnp.maximum(inter, jnp.max(dlog, axis=-1))
        s = jnp.einsum('bhtd,bhsd->bhts', qc, kc) * jnp.exp(dlog - m_t[..., None])
        w_prev = jnp.exp(inter - m_t)
        num = jnp.einsum('bhts,bhsv->bhtv', s, vc) + w_prev[..., None] * jnp.einsum('bhtd,bhdv->bhtv', qc, C)
        den = jnp.sum(s, axis=-1) + w_prev * jnp.einsum('bhtd,bhd->bht', qc, n)
        hout = num / jnp.maximum(jnp.abs(den), jnp.exp(-m_t))[..., None]
        g = bcum[..., -1:] - bcum + ic
        m_new = jnp.maximum(bcum[..., -1] + m, jnp.max(g, axis=-1))
        wk = jnp.exp(g - m_new[..., None])
        decay = jnp.exp(bcum[..., -1] + m - m_new)
        C_new = decay[..., None, None] * C + jnp.einsum('bhs,bhsd,bhsv->bhdv', wk, kc, vc)
        n_new = decay[..., None] * n + jnp.einsum('bhs,bhsd->bhd', wk, kc)
        return (C_new, n_new, m_new), hout

    (C, n, m), hs = lax.scan(step, (c0, n0, m0), tuple(to_chunks(a) for a in (q, k, v, i_log, f_log)))
    hseq = jnp.moveaxis(hs, 0, 2).reshape(b, h, t, dv)
    return hseq, C, n, m


def mla_prompt(q_nope, q_rope, c, k_r, w_kv_b):
    bx, t, h, _ = q_nope.shape
    kv = jnp.einsum('btc,chx->bthx', c, w_kv_b.reshape(KV_LORA, A_HEADS, A_NOPE + A_V))
    k = jnp.concatenate([kv[..., :A_NOPE], jnp.broadcast_to(k_r[:, :, None, :], (bx, t, h, A_ROPE))], axis=-1)
    v = kv[..., A_NOPE:]
    q = jnp.concatenate([q_nope, q_rope], axis=-1)
    nb = t // Q_BLOCK
    qb = jnp.moveaxis(q.reshape(bx, nb, Q_BLOCK, h, A_NOPE + A_ROPE), 1, 0)
    key_pos = jnp.arange(t)
    scale = (A_NOPE + A_ROPE) ** -0.5

    def block(args):
        qblk, bi = args
        q_pos = bi * Q_BLOCK + jnp.arange(Q_BLOCK)
        s = jnp.einsum('bqhd,bkhd->bhqk', qblk, k).astype(F32) * scale
        s = jnp.where(key_pos[None, :] <= q_pos[:, None], s, -jnp.inf)
        p = jax.nn.softmax(s, axis=-1).astype(v.dtype)
        return jnp.einsum('bhqk,bkhv->bqhv', p, v)

    out = lax.map(block, (qb, jnp.arange(nb)))
    return jnp.moveaxis(out, 0, 1).reshape(bx, t, h * A_V)


def mla_sample(q_nope, q_rope, c, k_r, w_kv_b, cache_ckv_l, cache_krope_l, page_table):
    bx, t, h, _ = q_nope.shape
    w = w_kv_b.reshape(KV_LORA, A_HEADS, A_NOPE + A_V)
    w_uk, w_uv = w[..., :A_NOPE], w[..., A_NOPE:]
    q_lat = jnp.einsum('bthn,chn->bthc', q_nope, w_uk)
    scale = (A_NOPE + A_ROPE) ** -0.5
    causal = jnp.tril(jnp.ones((t, t), dtype=bool))

    def one_seq(args):
        ql, qr, cn, kn, pages = args
        ckv = cache_ckv_l[pages].reshape(-1, KV_LORA)
        kr = cache_krope_l[pages].reshape(-1, A_ROPE)
        past = ckv.shape[0]
        s_past = (jnp.einsum('thc,sc->hts', ql, ckv) + jnp.einsum('thr,sr->hts', qr, kr)).astype(F32) * scale
        s_new = (jnp.einsum('thc,sc->hts', ql, cn) + jnp.einsum('thr,sr->hts', qr, kn)).astype(F32) * scale
        s_new = jnp.where(causal, s_new, -jnp.inf)
        p = jax.nn.softmax(jnp.concatenate([s_past, s_new], axis=-1), axis=-1).astype(ckv.dtype)
        o = jnp.einsum('hts,sc->thc', p[..., :past], ckv) + jnp.einsum('hts,sc->thc', p[..., past:], cn)
        return jnp.einsum('thc,chv->thv', o, w_uv).reshape(t, h * A_V)

    return lax.map(one_seq, (q_lat, q_rope, c, k_r, page_table))


def mixer_output(h_m, mo, mlstm_norm, att, w_out):
    bx, h, t, dv = h_m.shape
    hm = h_m.transpose(0, 2, 1, 3)
    hm = hm * lax.rsqrt(jnp.mean(hm * hm, axis=-1, keepdims=True) + EPS) * mlstm_norm.reshape(M_HEADS, M_V_DIM).astype(F32)
    hm = hm.reshape(bx, t, h * dv) * jax.nn.sigmoid(mo.astype(F32))
    cat = jnp.concatenate([hm.astype(att.dtype), att], axis=-1)
    return cat @ w_out


def grouped_experts(xf, ids, wts, w_gate, w_up, w_down):
    m, d = xf.shape
    a = m * TOP_K
    e_flat = ids.reshape(-1)
    tok = jnp.repeat(jnp.arange(m), TOP_K)
    order = jnp.argsort(e_flat)
    es, ts, ws = e_flat[order], tok[order], wts.reshape(-1)[order]
    counts = jnp.bincount(e_flat, length=N_EXPERTS)
    blk = min(MAX_BLOCK_ROWS, max(MIN_BLOCK_ROWS, 1 << max(0, (a // N_EXPERTS).bit_length() - 1)))
    padded = ((counts + blk - 1) // blk) * blk
    pad_end = jnp.cumsum(padded)
    pad_start = pad_end - padded
    start = jnp.cumsum(counts) - counts
    dest = pad_start[es] + jnp.arange(a) - start[es]
    n_blocks = -(-a // blk) + N_EXPERTS
    buf = jnp.zeros((n_blocks * blk, d), xf.dtype).at[dest].set(xf[ts])
    blk_expert = jnp.minimum(jnp.searchsorted(pad_end, jnp.arange(n_blocks) * blk, side='right'), N_EXPERTS - 1)

    def run(args):
        xb, e = args
        return (jax.nn.silu(xb @ w_gate[e]) * (xb @ w_up[e])) @ w_down[e]

    yb = lax.map(run, (buf.reshape(n_blocks, blk, d), blk_expert)).reshape(n_blocks * blk, d)
    return jnp.zeros((m, d), xf.dtype).at[ts].add(yb[dest] * ws[:, None].astype(xf.dtype))


def hier_moe(x, w_group, b_group, w_router, b_router, w_gate, w_up, w_down):
    bx, t, d = x.shape
    xf = x.reshape(-1, d)
    m = xf.shape[0]
    g_prob = jax.nn.softmax((xf @ w_group + b_group).astype(F32), axis=-1)
    g_w, g_idx = lax.top_k(g_prob, 1)
    e_logits = (xf @ w_router + b_router).astype(F32).reshape(m, N_GROUPS, EXPERTS_PER_GROUP)
    e_in = e_logits[jnp.arange(m), g_idx[:, 0]]
    top_logit, local = lax.top_k(e_in, TOP_K)
    e_w = jax.nn.softmax(top_logit, axis=-1)
    ids = g_idx * EXPERTS_PER_GROUP + local
    wts = g_w * e_w
    return grouped_experts(xf, ids, wts, w_gate, w_up, w_down).reshape(bx, t, d)


def channel_and_ple(h, p, norm_ffn, w_group, b_group, w_router, b_router, w_gate, w_up, w_down, w_ple_proj, w_ple_gate):
    h = h + hier_moe(rms_norm(h, norm_ffn), w_group, b_group, w_router, b_router, w_gate, w_up, w_down)
    gate = jax.nn.sigmoid((h @ w_ple_gate).astype(F32)).astype(h.dtype)
    return h + gate * (p.astype(h.dtype) @ w_ple_proj)


def setup_inputs(seed: int = 0) -> dict:
    key = jax.random.key(seed)
    ks = iter(jax.random.split(key, 40))

    def nrm(shape, scale=1.0):
        return jax.random.normal(next(ks), shape, F32) * scale

    n_pages = PAST_LEN // PAGE_SIZE
    n_pool = (DEC_BATCH * n_pages * 5) // 4
    x_prompt = nrm((BATCH, SEQ, D_MODEL))
    x_sample = nrm((DEC_BATCH, DEC_SEQ, D_MODEL))
    p_prompt = nrm((DEPTH, BATCH, SEQ, PLE_DIM))
    p_sample = nrm((DEPTH, DEC_BATCH, DEC_SEQ, PLE_DIM))
    cache_ckv = nrm((DEPTH, n_pool, PAGE_SIZE, KV_LORA))
    cache_krope = nrm((DEPTH, n_pool, PAGE_SIZE, A_ROPE))
    state_C = nrm((DEPTH, DEC_BATCH, M_HEADS, M_QK_DIM, M_V_DIM))
    state_n = nrm((DEPTH, DEC_BATCH, M_HEADS, M_QK_DIM))
    state_m = nrm((DEPTH, DEC_BATCH, M_HEADS))
    page_table = jax.random.permutation(next(ks), n_pool)[: DEC_BATCH * n_pages].reshape(DEC_BATCH, n_pages).astype(jnp.int32)
    return {
        "x_prompt": x_prompt, "x_sample": x_sample, "p_prompt": p_prompt, "p_sample": p_sample,
        "cache_ckv": cache_ckv, "cache_krope": cache_krope,
        "state_C": state_C, "state_n": state_n, "state_m": state_m, "page_table": page_table,
        "norm_mix": 1.0 + nrm((DEPTH, D_MODEL), 0.02),
        "w_in": nrm((DEPTH, D_MODEL, IN_COLS), D_MODEL ** -0.5),
        "b_igate": nrm((DEPTH, M_HEADS), 0.1),
        "b_fgate": 3.0 + nrm((DEPTH, M_HEADS), 0.1),
        "mlstm_norm": 1.0 + nrm((DEPTH, M_HEADS * M_V_DIM), 0.02),
        "q_a_norm": 1.0 + nrm((DEPTH, Q_LORA), 0.02),
        "w_q_b": nrm((DEPTH, Q_LORA, A_HEADS * (A_NOPE + A_ROPE)), Q_LORA ** -0.5),
        "kv_a_norm": 1.0 + nrm((DEPTH, KV_LORA), 0.02),
        "w_kv_b": nrm((DEPTH, KV_LORA, A_HEADS * (A_NOPE + A_V)), KV_LORA ** -0.5),
        "w_out": nrm((DEPTH, D_MODEL, D_MODEL), D_MODEL ** -0.5),
        "norm_ffn": 1.0 + nrm((DEPTH, D_MODEL), 0.02),
        "w_group": nrm((DEPTH, D_MODEL, N_GROUPS), D_MODEL ** -0.5),
        "b_group": nrm((DEPTH, N_GROUPS), 0.01),
        "w_router": nrm((DEPTH, D_MODEL, N_EXPERTS), D_MODEL ** -0.5),
        "b_router": nrm((DEPTH, N_EXPERTS), 0.01),
        "w_gate": nrm((DEPTH, N_EXPERTS, D_MODEL, D_EXPERT), D_MODEL ** -0.5),
        "w_up": nrm((DEPTH, N_EXPERTS, D_MODEL, D_EXPERT), D_MODEL ** -0.5),
        "w_down": nrm((DEPTH, N_EXPERTS, D_EXPERT, D_MODEL), D_EXPERT ** -0.5),
        "w_ple_proj": nrm((DEPTH, PLE_DIM, D_MODEL), PLE_DIM ** -0.5),
        "w_ple_gate": nrm((DEPTH, D_MODEL, D_MODEL), D_MODEL ** -0.5),
        "norm_final": 1.0 + nrm((D_MODEL,), 0.02),
    }


def reference(x_prompt, x_sample, p_prompt, p_sample, cache_ckv, cache_krope, state_C, state_n, state_m,
              page_table, norm_mix, w_in, b_igate, b_fgate, mlstm_norm, q_a_norm, w_q_b, kv_a_norm, w_kv_b,
              w_out, norm_ffn, w_group, b_group, w_router, b_router, w_gate, w_up, w_down, w_ple_proj,
              w_ple_gate, norm_final):
    bp, tp, _ = x_prompt.shape
    ts_len = x_sample.shape[1]
    past_len = page_table.shape[1] * cache_ckv.shape[2]
    pos_p = jnp.arange(tp)
    pos_s = past_len + jnp.arange(ts_len)
    hp, hs = x_prompt, x_sample
    ckv_p, kr_p, Cp_l, np_l, mp_l = [], [], [], [], []
    ckv_s, kr_s, Cs_l, ns_l, ms_l = [], [], [], [], []
    for l in range(DEPTH):
        (q, k, v, il, fl), mo, (qn, qr, c, kr) = mixer_inputs(
            hp, pos_p, norm_mix[l], w_in[l], b_igate[l], b_fgate[l], q_a_norm[l], w_q_b[l], kv_a_norm[l])
        c0 = jnp.zeros((bp, M_HEADS, M_QK_DIM, M_V_DIM), F32)
        n0 = jnp.zeros((bp, M_HEADS, M_QK_DIM), F32)
        m0 = jnp.zeros((bp, M_HEADS), F32)
        hm, Cn, nn, mn = mlstm_chunkwise(q, k, v, il, fl, c0, n0, m0)
        att = mla_prompt(qn, qr, c, kr, w_kv_b[l])
        hp = hp + mixer_output(hm, mo, mlstm_norm[l], att, w_out[l])
        hp = channel_and_ple(hp, p_prompt[l], norm_ffn[l], w_group[l], b_group[l], w_router[l], b_router[l],
                             w_gate[l], w_up[l], w_down[l], w_ple_proj[l], w_ple_gate[l])
        ckv_p.append(c); kr_p.append(kr); Cp_l.append(Cn); np_l.append(nn); mp_l.append(mn)
        (q, k, v, il, fl), mo, (qn, qr, c, kr) = mixer_inputs(
            hs, pos_s, norm_mix[l], w_in[l], b_igate[l], b_fgate[l], q_a_norm[l], w_q_b[l], kv_a_norm[l])
        hm, Cn, nn, mn = mlstm_chunkwise(q, k, v, il, fl, state_C[l].astype(F32), state_n[l].astype(F32),
                                         state_m[l].astype(F32))
        att = mla_sample(qn, qr, c, kr, w_kv_b[l], cache_ckv[l], cache_krope[l], page_table)
        hs = hs + mixer_output(hm, mo, mlstm_norm[l], att, w_out[l])
        hs = channel_and_ple(hs, p_sample[l], norm_ffn[l], w_group[l], b_group[l], w_router[l], b_router[l],
                             w_gate[l], w_up[l], w_down[l], w_ple_proj[l], w_ple_gate[l])
        ckv_s.append(c); kr_s.append(kr); Cs_l.append(Cn); ns_l.append(nn); ms_l.append(mn)
    y_prompt = rms_norm(hp, norm_final)
    y_sample = rms_norm(hs, norm_final)
    return (y_prompt, y_sample,
            jnp.stack(ckv_p), jnp.stack(kr_p), jnp.stack(Cp_l), jnp.stack(np_l), jnp.stack(mp_l),
            jnp.stack(ckv_s), jnp.stack(kr_s), jnp.stack(Cs_l), jnp.stack(ns_l), jnp.stack(ms_l))
```

```python
import functools
import math

import jax
import jax.numpy as jnp
from jax import lax
from jax.experimental import pallas as pl
from jax.experimental.pallas import tpu as pltpu

F32 = jnp.float32
BF16 = jnp.bfloat16
SDS = jax.ShapeDtypeStruct

EPS = 1e-6
GATE_CAP = 15.0
ROPE_THETA = 10000.0
A_NOPE = 128
TOP_K = 2

LANES = 128
SUBLANES = 8
VMEM_LIMIT = 56 * 1024 * 1024
NEG_INF = float("-inf")

NT_DIMS = (((1,), (1,)), ((), ()))


def _cparams(sem, vmem=VMEM_LIMIT):
    return pltpu.CompilerParams(dimension_semantics=sem, vmem_limit_bytes=vmem)


def _pick(n, pref, mult):
    t = min(pref, n) // mult * mult
    while t >= mult:
        if n % t == 0:
            return t
        t -= mult
    return n


def _rup(x, m):
    return (x + m - 1) // m * m


def _dot(a, b):
    return jnp.dot(a, b, preferred_element_type=F32)


def _dot_nt(a, b, precision=None):
    return lax.dot_general(a, b, NT_DIMS, preferred_element_type=F32, precision=precision)


def _rms(x, g):
    ms = jnp.mean(x * x, axis=-1, keepdims=True)
    return x * lax.rsqrt(ms + EPS) * g


def _inproj_kernel(x_ref, g_ref, w_ref, ws_ref, *rest):
    z_ref, zs_ref, xn_ref = rest[-3:]
    j = pl.program_id(1)

    @pl.when(j == 0)
    def _():
        xn = _rms(x_ref[...], g_ref[...]).astype(BF16)
        xn_ref[...] = xn
        zs_ref[...] = _dot(xn, ws_ref[...])

    z_ref[...] = _dot(xn_ref[...], w_ref[...])


def _in_proj(x, g, w_main, w_small, m_total, row0, prev):
    rows, d = x.shape
    nm = w_main.shape[1]
    tm = _pick(math.gcd(rows, row0) if row0 else rows, 512, 16)
    tn = _pick(nm, 512, LANES)
    ro = row0 // tm
    ins = [x, g, w_main, w_small]
    in_specs = [
        pl.BlockSpec((tm, d), lambda i, j: (i, 0)),
        pl.BlockSpec((1, d), lambda i, j: (0, 0)),
        pl.BlockSpec((d, tn), lambda i, j: (0, j)),
        pl.BlockSpec((d, LANES), lambda i, j: (0, 0)),
    ]
    aliases = {}
    if prev is not None:
        ins += list(prev)
        in_specs += [pl.BlockSpec(memory_space=pl.ANY)] * 2
        aliases = {4: 0, 5: 1}
    return pl.pallas_call(
        _inproj_kernel,
        grid=(rows // tm, nm // tn),
        in_specs=in_specs,
        out_specs=[pl.BlockSpec((tm, tn), lambda i, j: (i + ro, j)),
                   pl.BlockSpec((tm, LANES), lambda i, j: (i + ro, 0))],
        out_shape=[SDS((m_total, nm), F32), SDS((m_total, LANES), F32)],
        scratch_shapes=[pltpu.VMEM((tm, d), BF16)],
        input_output_aliases=aliases,
        compiler_params=_cparams(("arbitrary", "arbitrary")),
        name="in_proj",
    )(*ins)


def _rope_lanes(y, cos, sin_signed, width):
    half = 32
    lane = lax.broadcasted_iota(jnp.int32, y.shape, 1)
    first = (lane % (2 * half)) < half
    swapped = jnp.where(first, pltpu.roll(y, width - half, 1), pltpu.roll(y, half, 1))
    return y * cos + swapped * sin_signed


def _mlaq_kernel(n_heads, rope, aq_ref, c_ref, zs_ref, cos_ref, sin_ref, qg_ref, kvg_ref, wq_ref,
                 q_ref, ckv_ref, kr_ref):
    aqn = _rms(aq_ref[...], qg_ref[...]).astype(BF16)
    y = _dot(aqn, wq_ref[...])
    cos = cos_ref[...]
    sin = sin_ref[...]
    nope_w = n_heads * A_NOPE
    for h2 in range(n_heads // 2):
        yr = y[:, nope_w + h2 * LANES: nope_w + (h2 + 1) * LANES]
        rot = _rope_lanes(yr, cos, sin, LANES)
        for s in range(2):
            h = 2 * h2 + s
            q_ref[h, :, 0:A_NOPE] = y[:, h * A_NOPE:(h + 1) * A_NOPE].astype(BF16)
            q_ref[h, :, A_NOPE:A_NOPE + rope] = rot[:, s * rope:(s + 1) * rope].astype(BF16)
    ckv_ref[...] = _rms(c_ref[...], kvg_ref[...])
    kr = _rope_lanes(zs_ref[...], cos, sin, LANES)
    kr_ref[...] = kr[:, 0:rope]


def _mla_q(z, zs, cos, sin, qg, kvg, wq, n_heads, ql, kvl, rope, aq_off, c_off):
    m = z.shape[0]
    tm = _pick(m, 256, 16)
    qd = A_NOPE + rope
    return pl.pallas_call(
        functools.partial(_mlaq_kernel, n_heads, rope),
        grid=(m // tm,),
        in_specs=[
            pl.BlockSpec((tm, ql), lambda i: (i, aq_off // ql)),
            pl.BlockSpec((tm, kvl), lambda i: (i, c_off // kvl)),
            pl.BlockSpec((tm, LANES), lambda i: (i, 0)),
            pl.BlockSpec((tm, LANES), lambda i: (i, 0)),
            pl.BlockSpec((tm, LANES), lambda i: (i, 0)),
            pl.BlockSpec((1, ql), lambda i: (0, 0)),
            pl.BlockSpec((1, kvl), lambda i: (0, 0)),
            pl.BlockSpec((ql, n_heads * qd), lambda i: (0, 0)),
        ],
        out_specs=[
            pl.BlockSpec((n_heads, tm, qd), lambda i: (0, i, 0)),
            pl.BlockSpec((tm, kvl), lambda i: (i, 0)),
            pl.BlockSpec((tm, rope), lambda i: (i, 0)),
        ],
        out_shape=[SDS((n_heads, m, qd), BF16), SDS((m, kvl), F32), SDS((m, rope), F32)],
        compiler_params=_cparams(("arbitrary",)),
        name="mla_q",
    )(z, z, zs, cos, sin, qg, kvg, wq)


def _kv_kernel(n_heads, av, rope, c_ref, kr_ref, w_ref, k_ref, v_ref):
    y = _dot(c_ref[...].astype(BF16), w_ref[...])
    kr = kr_ref[...].astype(BF16)
    hw = A_NOPE + av
    for h in range(n_heads):
        k_ref[h, :, 0:A_NOPE] = y[:, h * hw: h * hw + A_NOPE].astype(BF16)
        k_ref[h, :, A_NOPE:A_NOPE + rope] = kr
        v_ref[h] = y[:, h * hw + A_NOPE:(h + 1) * hw].astype(BF16)


def _kv_expand(ckv, kr, w_kvb, rows, n_heads, av):
    kvl = ckv.shape[1]
    rope = kr.shape[1]
    tm = _pick(rows, 256, 16)
    return pl.pallas_call(
        functools.partial(_kv_kernel, n_heads, av, rope),
        grid=(rows // tm,),
        in_specs=[
            pl.BlockSpec((tm, kvl), lambda i: (i, 0)),
            pl.BlockSpec((tm, rope), lambda i: (i, 0)),
            pl.BlockSpec(w_kvb.shape, lambda i: (0, 0)),
        ],
        out_specs=[
            pl.BlockSpec((n_heads, tm, A_NOPE + rope), lambda i: (0, i, 0)),
            pl.BlockSpec((n_heads, tm, av), lambda i: (0, i, 0)),
        ],
        out_shape=[SDS((n_heads, rows, A_NOPE + rope), BF16), SDS((n_heads, rows, av), BF16)],
        compiler_params=_cparams(("arbitrary",)),
        name="kv_expand",
    )(ckv, kr, w_kvb)


def _flash_kernel(tq, tk, scale, q_ref, k_ref, v_ref, o_ref):
    qi = pl.program_id(2)
    q = q_ref[0]
    av = v_ref.shape[-1]
    row = qi * tq + lax.broadcasted_iota(jnp.int32, (tq, tk), 0)
    col0 = lax.broadcasted_iota(jnp.int32, (tq, tk), 1)

    def body(kb, carry):
        m, l, acc = carry
        start = pl.multiple_of(kb * tk, tk)
        k = k_ref[0, pl.ds(start, tk), :]
        v = v_ref[0, pl.ds(start, tk), :]
        s = _dot_nt(q, k) * scale
        s = jnp.where(col0 + kb * tk <= row, s, NEG_INF)
        m_new = jnp.maximum(m, jnp.max(s, axis=-1, keepdims=True))
        p = jnp.exp(s - m_new)
        alpha = jnp.exp(m - m_new)
        l = alpha * l + jnp.sum(p, axis=-1, keepdims=True)
        acc = alpha * acc + _dot(p.astype(BF16), v)
        return m_new, l, acc

    n_kb = (qi * tq + tq + tk - 1) // tk
    init = (jnp.full((tq, 1), NEG_INF, F32), jnp.zeros((tq, 1), F32), jnp.zeros((tq, av), F32))
    _, l, acc = lax.fori_loop(0, n_kb, body, init)
    o_ref[...] = (acc / l).astype(o_ref.dtype)


def _flash_prompt(q_hm, k_hm, v_hm, n_batch, seq, m_total):
    n_heads, _, qd = q_hm.shape
    av = v_hm.shape[-1]
    tq = _pick(seq, 256, 16)
    tk = _pick(seq, 512, 16)
    nq = seq // tq
    return pl.pallas_call(
        functools.partial(_flash_kernel, tq, tk, qd ** -0.5),
        grid=(n_batch, n_heads, nq),
        in_specs=[
            pl.BlockSpec((1, tq, qd), lambda b, h, i: (h, b * nq + i, 0)),
            pl.BlockSpec((1, seq, qd), lambda b, h, i: (h, b, 0)),
            pl.BlockSpec((1, seq, av), lambda b, h, i: (h, b, 0)),
        ],
        out_specs=pl.BlockSpec((tq, av), lambda b, h, i: (b * nq + i, h)),
        out_shape=SDS((m_total, n_heads * av), BF16),
        compiler_params=_cparams(("arbitrary", "arbitrary", "arbitrary")),
        name="flash_prompt",
    )(q_hm, k_hm, v_hm)


def _qlat_kernel(n_seq, ts, kvl, rope, q_ref, w_ref, o_ref):
    q = q_ref[0]
    qlat = _dot(q[:, 0:A_NOPE], w_ref[0])
    width = o_ref.shape[-1]
    full = jnp.concatenate(
        [qlat, q[:, A_NOPE:A_NOPE + rope].astype(F32),
         jnp.zeros((q.shape[0], width - kvl - rope), F32)], axis=1)
    o_ref[...] = full.reshape(n_seq, 1, ts, width)


def _q_latent(q_hm, w_ukt, rows0, n_seq, ts):
    n_heads, _, qd = q_hm.shape
    kvl = w_ukt.shape[-1]
    rope = qd - A_NOPE
    ms = n_seq * ts
    width = _rup(kvl + rope, LANES)
    return pl.pallas_call(
        functools.partial(_qlat_kernel, n_seq, ts, kvl, rope),
        grid=(n_heads,),
        in_specs=[
            pl.BlockSpec((1, ms, qd), lambda h: (h, rows0 // ms, 0)),
            pl.BlockSpec((1, A_NOPE, kvl), lambda h: (h, 0, 0)),
        ],
        out_specs=pl.BlockSpec((n_seq, 1, ts, width), lambda h: (0, h, 0, 0)),
        out_shape=SDS((n_seq, n_heads, ts, width), F32),
        compiler_params=_cparams(("arbitrary",)),
        name="q_latent",
    )(q_hm, w_ukt)


def _mla_sample_kernel(n_pages, chunk_pages, page, n_seq, ts, kvl, rope, scale,
                       pt_ref, q_ref, cn_ref, kn_ref, ckv_hbm, kr_hbm, o_ref,
                       kbuf, rbuf, sem, m_ref, l_ref, acc_ref):
    b = pl.program_id(0)
    n_chunks = n_pages // chunk_pages
    rows = q_ref.shape[1] * ts

    def chunk_copies(bb, cc, slot):
        copies = []
        for p in range(chunk_pages):
            pg = pt_ref[bb * n_pages + cc * chunk_pages + p]
            copies.append(pltpu.make_async_copy(
                ckv_hbm.at[pg], kbuf.at[slot, pl.ds(p * page, page)], sem.at[slot]))
            copies.append(pltpu.make_async_copy(
                kr_hbm.at[pg], rbuf.at[slot, pl.ds(p * page, page)], sem.at[slot]))
        return copies

    def start_chunk(bb, cc, slot):
        for cp in chunk_copies(bb, cc, slot):
            cp.start()

    @pl.when(b == 0)
    def _():
        start_chunk(0, 0, 0)

    q = q_ref[0].reshape(rows, q_ref.shape[-1])
    ql = q[:, 0:kvl].astype(BF16)
    qr = q[:, kvl:kvl + rope].astype(BF16)
    m_ref[...] = jnp.full(m_ref.shape, NEG_INF, F32)
    l_ref[...] = jnp.zeros(l_ref.shape, F32)
    acc_ref[...] = jnp.zeros(acc_ref.shape, F32)

    def update(s, values):
        m = m_ref[...]
        m_new = jnp.maximum(m, jnp.max(s, axis=-1, keepdims=True))
        p = jnp.exp(s - m_new)
        alpha = jnp.exp(m - m_new)
        l_ref[...] = alpha * l_ref[...] + jnp.sum(p, axis=-1, keepdims=True)
        acc_ref[...] = alpha * acc_ref[...] + _dot(p.astype(BF16), values)
        m_ref[...] = m_new

    def body(c, carry):
        g = b * n_chunks + c
        slot = g % 2

        @pl.when(c + 1 < n_chunks)
        def _():
            start_chunk(b, c + 1, 1 - slot)

        @pl.when((c + 1 == n_chunks) & (b + 1 < n_seq))
        def _():
            start_chunk(b + 1, 0, 1 - slot)

        for cp in chunk_copies(b, c, slot):
            cp.wait()
        k = kbuf[slot].astype(BF16)
        r = rbuf[slot].astype(BF16)
        s = (_dot_nt(ql, k) + _dot_nt(qr, r)) * scale
        update(s, k)
        return carry

    lax.fori_loop(0, n_chunks, body, 0)

    pad = 2 * SUBLANES - ts
    cn = jnp.concatenate([cn_ref[...], jnp.zeros((pad, kvl), F32)], axis=0).astype(BF16)
    kn = jnp.concatenate([kn_ref[...], jnp.zeros((pad, rope), F32)], axis=0).astype(BF16)
    s = (_dot_nt(ql, cn) + _dot_nt(qr, kn)) * scale
    tok = lax.broadcasted_iota(jnp.int32, s.shape, 0) % ts
    col = lax.broadcasted_iota(jnp.int32, s.shape, 1)
    s = jnp.where(col <= tok, s, NEG_INF)
    update(s, cn)
    o = acc_ref[...] / l_ref[...]
    o_ref[0] = o.reshape(o_ref.shape[1:])


def _mla_sample(page_table, qs, ckv_s, kr_s, cache_ckv, cache_kr, scale):
    n_seq, n_heads, ts, width = qs.shape
    n_pages = page_table.shape[1]
    page, kvl = cache_ckv.shape[1:]
    rope = cache_kr.shape[-1]
    chunk_pages = _pick(n_pages, 8, 1)
    rows = n_heads * ts
    keys = chunk_pages * page
    kern = functools.partial(_mla_sample_kernel, n_pages, chunk_pages, page, n_seq, ts, kvl, rope, scale)
    return pl.pallas_call(
        kern,
        grid_spec=pltpu.PrefetchScalarGridSpec(
            num_scalar_prefetch=1,
            grid=(n_seq,),
            in_specs=[
                pl.BlockSpec((1, n_heads, ts, width), lambda b, pt: (b, 0, 0, 0)),
                pl.BlockSpec((ts, kvl), lambda b, pt: (b, 0)),
                pl.BlockSpec((ts, rope), lambda b, pt: (b, 0)),
                pl.BlockSpec(memory_space=pl.ANY),
                pl.BlockSpec(memory_space=pl.ANY),
            ],
            out_specs=pl.BlockSpec((1, n_heads, ts, kvl), lambda b, pt: (b, 0, 0, 0)),
            scratch_shapes=[
                pltpu.VMEM((2, keys, kvl), F32),
                pltpu.VMEM((2, keys, rope), F32),
                pltpu.SemaphoreType.DMA((2,)),
                pltpu.VMEM((rows, 1), F32),
                pltpu.VMEM((rows, 1), F32),
                pltpu.VMEM((rows, kvl), F32),
            ],
        ),
        out_shape=SDS((n_seq, n_heads, ts, kvl), F32),
        compiler_params=_cparams(("arbitrary",)),
        name="mla_sample",
    )(page_table.reshape(-1), qs, ckv_s, kr_s, cache_ckv, cache_kr)


def _ouv_kernel(o_ref, w_ref, prev_ref, att_ref):
    del prev_ref
    o = o_ref[...]
    o = o.reshape(o.shape[0] * o.shape[2], o.shape[3]).astype(BF16)
    att_ref[...] = _dot(o, w_ref[0]).astype(att_ref.dtype)


def _o_uv(o_s, w_uv, att, rows0):
    n_seq, n_heads, ts, kvl = o_s.shape
    av = w_uv.shape[-1]
    ms = n_seq * ts
    return pl.pallas_call(
        _ouv_kernel,
        grid=(n_heads,),
        in_specs=[
            pl.BlockSpec((n_seq, 1, ts, kvl), lambda h: (0, h, 0, 0)),
            pl.BlockSpec((1, kvl, av), lambda h: (h, 0, 0)),
            pl.BlockSpec(memory_space=pl.ANY),
        ],
        out_specs=pl.BlockSpec((ms, av), lambda h: (rows0 // ms, h)),
        out_shape=SDS(att.shape, att.dtype),
        input_output_aliases={2: 0},
        compiler_params=_cparams(("arbitrary",)),
        name="o_uv",
    )(o_s, w_uv, att)


def _mlstm_kernel(n_sub, L, n_heads, dk, dv, has_state, *refs):
    if has_state:
        (q_ref, k_ref, v_ref, mo_ref, zs_ref, bias_ref, norm_ref, c0_ref, n0_ref, m0_ref, _prev,
         hm_ref, c_ref, n_ref, m_ref) = refs
    else:
        (q_ref, k_ref, v_ref, mo_ref, zs_ref, bias_ref, norm_ref,
         hm_ref, c_ref, n_ref, m_ref) = refs
    ci = pl.program_id(1)
    Lk = _rup(L, LANES)

    @pl.when(ci == 0)
    def _():
        if has_state:
            c_ref[...] = c0_ref[...]
            n_ref[...] = n0_ref[...]
            m_ref[...] = m0_ref[...]
        else:
            c_ref[...] = jnp.zeros(c_ref.shape, F32)
            n_ref[...] = jnp.zeros(n_ref.shape, F32)
            m_ref[...] = jnp.zeros(m_ref.shape, F32)

    def pad_rows(a):
        if Lk == L:
            return a
        return jnp.concatenate([a, jnp.zeros((Lk - L, a.shape[1]), a.dtype)], axis=0)

    row = lax.broadcasted_iota(jnp.int32, (L, Lk), 0)
    col = lax.broadcasted_iota(jnp.int32, (L, Lk), 1)
    causal = col <= row
    tril = causal.astype(F32)
    sel_r = lax.broadcasted_iota(jnp.int32, (SUBLANES, LANES), 0)
    sel_c = lax.broadcasted_iota(jnp.int32, (SUBLANES, LANES), 1)
    lane0 = LANES // 2
    sel = ((sel_c == sel_r + lane0) & (sel_r < n_heads)).astype(F32)
    eye = (lax.broadcasted_iota(jnp.int32, (dk, dk), 0)
           == lax.broadcasted_iota(jnp.int32, (dk, dk), 1)).astype(BF16)
    hi = lax.Precision.HIGHEST

    for sq in range(n_sub):
        rs = slice(sq * L, (sq + 1) * L)
        pre = zs_ref[rs, :] + bias_ref[...]
        cap = GATE_CAP * jnp.tanh(pre / GATE_CAP)
        i_all = cap
        f_all = jnp.minimum(cap, 0.0) - jnp.log1p(jnp.exp(-jnp.abs(cap)))
        bcum_all = jnp.dot(tril, pad_rows(f_all), preferred_element_type=F32, precision=hi)
        r_all = i_all - pltpu.roll(bcum_all, LANES - n_heads, 1)
        r_rows = _dot_nt(sel, pad_rows(r_all), precision=hi)
        for h in range(n_heads):
            i_col = i_all[:, lane0 + h: lane0 + h + 1]
            b_col = bcum_all[:, lane0 + n_heads + h: lane0 + n_heads + h + 1]
            r_row = r_rows[h:h + 1, :]
            b_tot = b_col[L - 1:L, :]
            m_prev = m_ref[sq, :, h:h + 1]
            c_prev = c_ref[sq, h]
            n_prev = n_ref[sq, h:h + 1, :]

            q = q_ref[rs, h * dk:(h + 1) * dk] * (dk ** -0.5)
            k = k_ref[rs, h * dk:(h + 1) * dk]
            v = v_ref[rs, h * dv:(h + 1) * dv]
            qb = q.astype(BF16)
            kpb = pad_rows(k).astype(BF16)
            vpb = pad_rows(v).astype(BF16)

            dlog = jnp.where(causal, b_col + r_row, NEG_INF)
            inter = b_col + m_prev
            m_t = jnp.maximum(inter, jnp.max(dlog, axis=-1, keepdims=True))
            s = _dot_nt(qb, kpb) * jnp.exp(dlog - m_t)
            w_prev = jnp.exp(inter - m_t)
            num = _dot(s.astype(BF16), vpb) + w_prev * _dot(qb, c_prev.astype(BF16))
            den = (jnp.sum(s, axis=-1, keepdims=True)
                   + w_prev * jnp.sum(q * n_prev, axis=-1, keepdims=True))
            hout = num / jnp.maximum(jnp.abs(den), jnp.exp(-m_t))

            g = b_tot - b_col + i_col
            m_new = jnp.maximum(b_tot + m_prev, jnp.max(g, axis=0, keepdims=True))
            wk = jnp.exp(g - m_new)
            decay = jnp.exp(b_tot + m_prev - m_new)
            kw = wk * k
            kw_t = _dot_nt(eye, pad_rows(kw).astype(BF16)).astype(BF16)
            c_ref[sq, h] = decay * c_prev + _dot(kw_t, vpb)
            n_ref[sq, h:h + 1, :] = decay * n_prev + jnp.sum(kw, axis=0, keepdims=True)
            m_ref[sq, :, h:h + 1] = m_new

            hn = _rms(hout, norm_ref[:, h * dv:(h + 1) * dv])
            hn = hn * jax.nn.sigmoid(mo_ref[rs, h * dv:(h + 1) * dv])
            hm_ref[rs, h * dv:(h + 1) * dv] = hn.astype(hm_ref.dtype)


def _mlstm(z, zs, bias, norm, n_seq, seq, L, n_sub, rows0, n_heads, dk, dv, m_total, state, prev_hm):
    wq = n_heads * dk
    wv = n_heads * dv
    n_chunks = seq // L
    rb = n_sub * L
    has_state = state is not None
    if has_state:
        assert n_chunks == 1
    ro = rows0 // rb

    def rmap(cb):
        return lambda b, c: (ro + b * n_chunks + c, cb)

    ins = [z, z, z, z, zs, bias, norm]
    in_specs = [
        pl.BlockSpec((rb, wq), rmap(0)),
        pl.BlockSpec((rb, wq), rmap(1)),
        pl.BlockSpec((rb, wv), rmap(1)),
        pl.BlockSpec((rb, wv), rmap(2)),
        pl.BlockSpec((rb, LANES), rmap(0)),
        pl.BlockSpec((1, LANES), lambda b, c: (0, 0)),
        pl.BlockSpec((1, wv), lambda b, c: (0, 0)),
    ]
    c_spec = pl.BlockSpec((n_sub, n_heads, dk, dv), lambda b, c: (b, 0, 0, 0))
    n_spec = pl.BlockSpec((n_sub, n_heads, dk), lambda b, c: (b, 0, 0))
    m_spec = pl.BlockSpec((n_sub, 1, n_heads), lambda b, c: (b, 0, 0))
    aliases = {}
    if has_state:
        ins += [state[0], state[1], state[2], prev_hm]
        in_specs += [c_spec, n_spec, m_spec, pl.BlockSpec(memory_space=pl.ANY)]
        aliases = {10: 0}
    return pl.pallas_call(
        functools.partial(_mlstm_kernel, n_sub, L, n_heads, dk, dv, has_state),
        grid=(n_seq // n_sub, n_chunks),
        in_specs=in_specs,
        out_specs=[pl.BlockSpec((rb, wv), rmap(0)), c_spec, n_spec, m_spec],
        out_shape=[SDS((m_total, wv), BF16), SDS((n_seq, n_heads, dk, dv), F32),
                   SDS((n_seq, n_heads, dk), F32), SDS((n_seq, 1, n_heads), F32)],
        input_output_aliases=aliases,
        compiler_params=_cparams(("arbitrary", "arbitrary")),
        name="mlstm_state" if has_state else "mlstm_prompt",
    )(*ins)


def _outproj_kernel(hm_ref, att_ref, w1_ref, w2_ref, x_ref, *rest):
    o_ref = rest[-1]
    o_ref[...] = x_ref[...] + _dot(hm_ref[...], w1_ref[...]) + _dot(att_ref[...], w2_ref[...])


def _out_proj(hm, att, w1, w2, x, rows0, m_total, prev):
    rows, d = x.shape
    k1, k2 = hm.shape[1], att.shape[1]
    tm = _pick(math.gcd(rows, rows0) if rows0 else rows, 512, 16)
    tn = _pick(d, 512, LANES)
    ro = rows0 // tm
    ins = [hm, att, w1, w2, x]
    in_specs = [
        pl.BlockSpec((tm, k1), lambda i, j: (i + ro, 0)),
        pl.BlockSpec((tm, k2), lambda i, j: (i + ro, 0)),
        pl.BlockSpec((k1, tn), lambda i, j: (0, j)),
        pl.BlockSpec((k2, tn), lambda i, j: (0, j)),
        pl.BlockSpec((tm, tn), lambda i, j: (i, j)),
    ]
    aliases = {}
    if prev is not None:
        ins.append(prev)
        in_specs.append(pl.BlockSpec(memory_space=pl.ANY))
        aliases = {5: 0}
    return pl.pallas_call(
        _outproj_kernel,
        grid=(rows // tm, d // tn),
        in_specs=in_specs,
        out_specs=pl.BlockSpec((tm, tn), lambda i, j: (i + ro, j)),
        out_shape=SDS((m_total, d), F32),
        input_output_aliases=aliases,
        compiler_params=_cparams(("arbitrary", "arbitrary")),
        name="out_proj",
    )(*ins)


def _row_pitch(d):
    return d // LANES + SUBLANES


def _store_token_rows(ref, val, first_chunk, pitch):
    tm = val.shape[0]
    for cc in range(val.shape[1] // LANES):
        ref[pl.ds(first_chunk + cc, tm, stride=pitch), :] = val[:, cc * LANES:(cc + 1) * LANES]


def _zero_token_pad(ref, tm, n_chunks, pitch):
    for c in range(n_chunks, pitch):
        ref[pl.ds(c, tm, stride=pitch), :] = jnp.zeros((tm, LANES), F32)


def _load_token_rows(ref, tm, n_chunks, pitch, base=0):
    return [ref[pl.ds(base + c, tm, stride=pitch), :] for c in range(n_chunks)]


def _router_kernel(pitch, h_ref, g_ref, w_ref, b_ref, xp_ref, lg_ref):
    xn = _rms(h_ref[...], g_ref[...])
    tm, d = xn.shape
    _store_token_rows(xp_ref, xn, 0, pitch)
    _zero_token_pad(xp_ref, tm, d // LANES, pitch)
    lg_ref[...] = jnp.dot(xn, w_ref[...], preferred_element_type=F32,
                          precision=lax.Precision.HIGHEST) + b_ref[...]


def _router(h, g, w_rg, b_rg):
    m, d = h.shape
    tm = _pick(m, 256, 16)
    pitch = _row_pitch(d)
    return pl.pallas_call(
        functools.partial(_router_kernel, pitch),
        grid=(m // tm,),
        in_specs=[
            pl.BlockSpec((tm, d), lambda i: (i, 0)),
            pl.BlockSpec((1, d), lambda i: (0, 0)),
            pl.BlockSpec((d, LANES), lambda i: (0, 0)),
            pl.BlockSpec((1, LANES), lambda i: (0, 0)),
        ],
        out_specs=[pl.BlockSpec((tm * pitch, LANES), lambda i: (i, 0)),
                   pl.BlockSpec((tm, LANES), lambda i: (i, 0))],
        out_shape=[SDS((m * pitch, LANES), F32), SDS((m, LANES), F32)],
        compiler_params=_cparams(("arbitrary",)),
        name="ffn_router",
    )(h, g, w_rg, b_rg)


def _dispatch_kernel(tile, pitch, n_chunks, nu_ref, nv_ref, idx_ref, xp_hbm, o_ref, stage, sem):
    i = pl.program_id(0)

    def copy(r):
        src = pl.multiple_of(idx_ref[i * tile + r] * pitch, SUBLANES)
        dst = pl.multiple_of(r * pitch, SUBLANES)
        return pltpu.make_async_copy(
            xp_hbm.at[pl.ds(src, n_chunks)], stage.at[pl.ds(dst, n_chunks)], sem)

    @pl.when(i == 0)
    def _():
        stage[...] = jnp.zeros(stage.shape, F32)

    @pl.when(i < nu_ref[0])
    def _():
        def start(r, c):
            copy(r).start()
            return c

        def wait(r, c):
            copy(r).wait()
            return c

        lax.fori_loop(0, nv_ref[i], start, 0)
        lax.fori_loop(0, nv_ref[i], wait, 0)
        for c, v in enumerate(_load_token_rows(stage, tile, n_chunks, pitch)):
            o_ref[:, c * LANES:(c + 1) * LANES] = v.astype(BF16)


def _dispatch(xp, idx, n_used_tiles, n_valid, tile, d):
    n = idx.shape[0]
    pitch = _row_pitch(d)

    def blk(i, nu, nv, ix):
        return (jnp.minimum(i, nu[0] - 1), 0)

    return pl.pallas_call(
        functools.partial(_dispatch_kernel, tile, pitch, d // LANES),
        grid_spec=pltpu.PrefetchScalarGridSpec(
            num_scalar_prefetch=3,
            grid=(n // tile,),
            in_specs=[pl.BlockSpec(memory_space=pl.ANY)],
            out_specs=pl.BlockSpec((tile, d), blk),
            scratch_shapes=[pltpu.VMEM((tile * pitch, LANES), F32), pltpu.SemaphoreType.DMA(())],
        ),
        out_shape=SDS((n, d), BF16),
        compiler_params=_cparams(("arbitrary",)),
        name="moe_dispatch",
    )(n_used_tiles, n_valid, idx, xp)


def _moe_up_kernel(be_ref, nu_ref, x_ref, wg_ref, wu_ref, hid_ref):
    b = pl.program_id(0)

    @pl.when(b < nu_ref[0])
    def _():
        x = x_ref[...]
        tf = wg_ref.shape[-1]
        w_gu = jnp.concatenate([wg_ref[0].astype(BF16), wu_ref[0].astype(BF16)], axis=1)
        gu = _dot(x, w_gu)
        hid_ref[...] = (jax.nn.silu(gu[:, 0:tf]) * gu[:, tf:2 * tf]).astype(BF16)


def _moe_down_kernel(pitch, n_chunks, be_ref, nu_ref, hid_ref, wd_ref, yp_ref):
    b = pl.program_id(0)
    j = pl.program_id(1)

    @pl.when(b < nu_ref[0])
    def _():
        tm = hid_ref.shape[0]
        tn = wd_ref.shape[-1]

        @pl.when(j == 0)
        def _():
            _zero_token_pad(yp_ref, tm, n_chunks, pitch)

        y = _dot(hid_ref[...], wd_ref[0].astype(BF16))
        _store_token_rows(yp_ref, y, j * (tn // LANES), pitch)


def _moe_ffn(x_sorted, blk_expert, n_used, w_gate, w_up, w_down, tm):
    a_pad, d = x_sorted.shape
    n_exp, _, de = w_gate.shape
    n_blocks = a_pad // tm
    pitch = _row_pitch(d)

    def blk(b, nu):
        return jnp.minimum(b, nu[0] - 1)

    def frozen(b, f, nu, last):
        return jnp.where(b < nu[0], f, last)

    tf = _pick(de, 256, LANES)
    nf = de // tf
    hid = pl.pallas_call(
        _moe_up_kernel,
        grid_spec=pltpu.PrefetchScalarGridSpec(
            num_scalar_prefetch=2,
            grid=(n_blocks, nf),
            in_specs=[
                pl.BlockSpec((tm, d), lambda b, f, be, nu: (blk(b, nu), 0)),
                pl.BlockSpec((1, d, tf), lambda b, f, be, nu: (be[b], 0, frozen(b, f, nu, nf - 1))),
                pl.BlockSpec((1, d, tf), lambda b, f, be, nu: (be[b], 0, frozen(b, f, nu, nf - 1))),
            ],
            out_specs=pl.BlockSpec((tm, tf), lambda b, f, be, nu: (blk(b, nu), frozen(b, f, nu, nf - 1))),
        ),
        out_shape=SDS((a_pad, de), BF16),
        compiler_params=_cparams(("arbitrary", "arbitrary")),
        name="moe_up",
    )(blk_expert, n_used, x_sorted, w_gate, w_up)

    tn = _pick(d, 1024, LANES)
    nj = d // tn
    return pl.pallas_call(
        functools.partial(_moe_down_kernel, pitch, d // LANES),
        grid_spec=pltpu.PrefetchScalarGridSpec(
            num_scalar_prefetch=2,
            grid=(n_blocks, nj),
            in_specs=[
                pl.BlockSpec((tm, de), lambda b, j, be, nu: (blk(b, nu), 0)),
                pl.BlockSpec((1, de, tn), lambda b, j, be, nu: (be[b], 0, frozen(b, j, nu, nj - 1))),
            ],
            out_specs=pl.BlockSpec((tm * pitch, LANES), lambda b, j, be, nu: (blk(b, nu), 0)),
        ),
        out_shape=SDS((a_pad * pitch, LANES), F32),
        compiler_params=_cparams(("arbitrary", "arbitrary")),
        name="moe_down",
    )(blk_expert, n_used, hid, w_down)


def _combine_kernel(tile, pitch, n_chunks, pos_ref, h_ref, wt_ref, yp_hbm, h2_ref, h2b_ref, ybuf, sem):
    i = pl.program_id(0)

    def copy(r):
        src = pl.multiple_of(pos_ref[i * TOP_K * tile + r] * pitch, SUBLANES)
        dst = pl.multiple_of(r * pitch, SUBLANES)
        return pltpu.make_async_copy(
            yp_hbm.at[pl.ds(src, n_chunks)], ybuf.at[pl.ds(dst, n_chunks)], sem)

    def start(r, c):
        copy(r).start()
        return c

    def wait(r, c):
        copy(r).wait()
        return c

    lax.fori_loop(0, TOP_K * tile, start, 0)
    lax.fori_loop(0, TOP_K * tile, wait, 0)
    wt = wt_ref[...]
    for c in range(n_chunks):
        cs = slice(c * LANES, (c + 1) * LANES)
        h2 = h_ref[:, cs]
        for kk in range(TOP_K):
            h2 = h2 + wt[:, kk:kk + 1] * ybuf[pl.ds(kk * tile * pitch + c, tile, stride=pitch), :]
        h2_ref[:, cs] = h2
        h2b_ref[:, cs] = h2.astype(BF16)


def _combine(h, yp, pos, wts):
    m, d = h.shape
    tile = _pick(m, 256, 16)
    n_tiles = m // tile
    pitch = _row_pitch(d)
    pos_tiled = pos.reshape(n_tiles, tile, TOP_K).transpose(0, 2, 1).reshape(-1)
    return pl.pallas_call(
        functools.partial(_combine_kernel, tile, pitch, d // LANES),
        grid_spec=pltpu.PrefetchScalarGridSpec(
            num_scalar_prefetch=1,
            grid=(n_tiles,),
            in_specs=[
                pl.BlockSpec((tile, d), lambda i, p: (i, 0)),
                pl.BlockSpec((tile, TOP_K), lambda i, p: (i, 0)),
                pl.BlockSpec(memory_space=pl.ANY),
            ],
            out_specs=[pl.BlockSpec((tile, d), lambda i, p: (i, 0)),
                       pl.BlockSpec((tile, d), lambda i, p: (i, 0))],
            scratch_shapes=[pltpu.VMEM((TOP_K * tile * pitch, LANES), F32), pltpu.SemaphoreType.DMA(())],
        ),
        out_shape=[SDS((m, d), F32), SDS((m, d), BF16)],
        compiler_params=_cparams(("arbitrary",)),
        name="moe_combine",
    )(pos_tiled, h, wts, yp)


def _ple_kernel(final_norm, nj, hb_ref, h_ref, p_ref, wg_ref, wp_ref, g_ref, o_ref, acc_ref):
    j = pl.program_id(1)
    gate = jax.nn.sigmoid(_dot(hb_ref[...], wg_ref[...]))
    proj = _dot(p_ref[...].astype(BF16), wp_ref[...])
    acc_ref[j] = h_ref[...] + gate * proj

    @pl.when(j == nj - 1)
    def _():
        tn = acc_ref.shape[-1]
        if final_norm:
            ssq = jnp.zeros((acc_ref.shape[1], 1), F32)
            for c in range(nj):
                a = acc_ref[c]
                ssq = ssq + jnp.sum(a * a, axis=-1, keepdims=True)
            rs = lax.rsqrt(ssq / (nj * tn) + EPS)
            for c in range(nj):
                o_ref[:, c * tn:(c + 1) * tn] = acc_ref[c] * rs * g_ref[:, c * tn:(c + 1) * tn]
        else:
            for c in range(nj):
                o_ref[:, c * tn:(c + 1) * tn] = acc_ref[c]


def _ple(h2b, h2, p, w_gate, w_proj, g, rows0, rows, final_norm):
    d = h2.shape[1]
    pd = p.shape[1]
    tm = _pick(math.gcd(rows, rows0) if rows0 else rows, 512, 16)
    tn = _pick(d, 512, LANES)
    nj = d // tn
    ro = rows0 // tm
    return pl.pallas_call(
        functools.partial(_ple_kernel, final_norm, nj),
        grid=(rows // tm, nj),
        in_specs=[
            pl.BlockSpec((tm, d), lambda i, j: (i + ro, 0)),
            pl.BlockSpec((tm, tn), lambda i, j: (i + ro, j)),
            pl.BlockSpec((tm, pd), lambda i, j: (i, 0)),
            pl.BlockSpec((d, tn), lambda i, j: (0, j)),
            pl.BlockSpec((pd, tn), lambda i, j: (0, j)),
            pl.BlockSpec((1, d), lambda i, j: (0, 0)),
        ],
        out_specs=pl.BlockSpec((tm, d), lambda i, j: (i, 0)),
        out_shape=SDS((rows, d), F32),
        scratch_shapes=[pltpu.VMEM((nj, tm, tn), F32)],
        compiler_params=_cparams(("arbitrary", "arbitrary")),
        name="ple_gate",
    )(h2b, h2, p, w_gate, w_proj, g)


def _route(logits, n_groups, epg, tm, tile):
    m = logits.shape[0]
    n_exp = n_groups * epg
    g_prob = jax.nn.softmax(logits[:, :n_groups], axis=-1)
    g_w, g_idx = lax.top_k(g_prob, 1)
    e_logits = logits[:, n_groups:n_groups + n_exp].reshape(m, n_groups, epg)
    e_in = jnp.take_along_axis(e_logits, g_idx[:, :, None], axis=1)[:, 0]
    top_logit, local = lax.top_k(e_in, TOP_K)
    e_w = jax.nn.softmax(top_logit, axis=-1)
    ids = g_idx * epg + local
    wts = g_w * e_w

    a = m * TOP_K
    e_flat = ids.reshape(-1).astype(jnp.int32)
    onehot = (e_flat[:, None] == jnp.arange(n_exp, dtype=jnp.int32)[None, :]).astype(jnp.int32)
    counts = jnp.sum(onehot, axis=0)
    rank = jnp.sum((jnp.cumsum(onehot, axis=0) - onehot) * onehot, axis=1)
    padded = (counts + tm - 1) // tm * tm
    pad_end = jnp.cumsum(padded)
    pad_start = pad_end - padded
    dest = (pad_start[e_flat] + rank).astype(jnp.int32)
    n_blocks = -(-a // tm) + n_exp
    tok = jnp.arange(a, dtype=jnp.int32) // TOP_K
    row_src = jnp.zeros((n_blocks * tm,), jnp.int32).at[dest].set(tok)
    n_used = (pad_end[-1] // tm).astype(jnp.int32)
    blk_id = jnp.arange(n_blocks, dtype=jnp.int32)
    blk_expert = jnp.minimum(jnp.searchsorted(pad_end, blk_id * tm, side="right"), n_exp - 1)
    last = blk_expert[jnp.maximum(n_used - 1, 0)]
    blk_expert = jnp.where(blk_id < n_used, blk_expert, last).astype(jnp.int32)
    tile_id = jnp.arange(n_blocks * tm // tile, dtype=jnp.int32)
    tile_e = blk_expert[tile_id * tile // tm]
    valid_end = (pad_start + counts)[tile_e]
    n_valid = jnp.clip(valid_end - tile_id * tile, 0, tile)
    n_valid = jnp.where(tile_id * tile < pad_end[-1], n_valid, 0).astype(jnp.int32)
    n_used_tiles = (n_used * (tm // tile)).reshape(1)
    return row_src, dest.reshape(m, TOP_K), wts, blk_expert, n_used.reshape(1), n_used_tiles, n_valid


def kernel(x_prompt, x_sample, p_prompt, p_sample, cache_ckv, cache_krope, state_C, state_n, state_m, page_table, norm_mix, w_in, b_igate, b_fgate, mlstm_norm, q_a_norm, w_q_b, kv_a_norm, w_kv_b, w_out, norm_ffn, w_group, b_group, w_router, b_router, w_gate, w_up, w_down, w_ple_proj, w_ple_gate, norm_final):
    depth = w_in.shape[0]
    bp, seq, d = x_prompt.shape
    n_dec, ts, _ = x_sample.shape
    mp, ms = bp * seq, n_dec * ts
    m = mp + ms
    hm_heads = b_igate.shape[-1]
    dk, dv = state_n.shape[-1], state_C.shape[-1]
    ql, kvl, rope = q_a_norm.shape[-1], kv_a_norm.shape[-1], cache_krope.shape[-1]
    att_w = d - hm_heads * dv
    a_heads = (w_q_b.shape[-1] - w_kv_b.shape[-1] + att_w) // rope
    av = att_w // a_heads
    n_groups, n_exp = w_group.shape[-1], w_router.shape[-1]
    epg = n_exp // n_groups
    n_pages, page = page_table.shape[1], cache_ckv.shape[2]
    past = n_pages * page
    wq, wv = hm_heads * dk, hm_heads * dv
    assert rope == LANES // 2 and a_heads % 2 == 0 and w_q_b.shape[-1] == a_heads * (A_NOPE + rope)
    assert dv == 2 * dk and mp % ms == 0 and n_groups + n_exp <= LANES and 2 * hm_heads <= LANES // 2
    moe_tm = 512
    moe_tile = 256

    half = rope // 2
    inv = 1.0 / (ROPE_THETA ** (jnp.arange(half, dtype=F32) / half))
    pos = jnp.concatenate([jnp.tile(jnp.arange(seq), bp), jnp.tile(past + jnp.arange(ts), n_dec)])
    ang = pos.astype(F32)[:, None] * inv[None, :]
    cos_t = jnp.tile(jnp.cos(ang), (1, LANES // half))
    sin_t = jnp.tile(jnp.concatenate([-jnp.sin(ang), jnp.sin(ang)], axis=1), (1, LANES // rope))

    hp = x_prompt.reshape(mp, d)
    hs = x_sample.reshape(ms, d)
    outs = {k: [] for k in ("ckv_p", "kr_p", "C_p", "n_p", "m_p", "ckv_s", "kr_s", "C_s", "n_s", "m_s")}
    for l in range(depth):
        wl = w_in[l]
        o_aq = 2 * wq + 2 * wv + 2 * hm_heads
        o_c = o_aq + ql
        o_kr = o_c + kvl
        w_main = jnp.concatenate([wl[:, :2 * wq + 2 * wv], wl[:, o_aq:o_kr]], axis=1).astype(BF16)
        n_small = rope + 2 * hm_heads
        w_small = jnp.concatenate(
            [wl[:, o_kr:o_kr + rope], wl[:, 2 * wq + 2 * wv:o_aq], jnp.zeros((d, LANES - n_small), F32)],
            axis=1).astype(BF16)
        gate_bias = jnp.concatenate(
            [jnp.zeros((rope,), F32), b_igate[l], b_fgate[l], jnp.zeros((LANES - n_small,), F32)]).reshape(1, LANES)
        wqb = w_q_b[l].reshape(ql, a_heads, A_NOPE + rope)
        wq_re = jnp.concatenate(
            [wqb[:, :, :A_NOPE].reshape(ql, -1), wqb[:, :, A_NOPE:].reshape(ql, -1)], axis=1).astype(BF16)
        wkvb = w_kv_b[l].astype(BF16)
        wkv3 = w_kv_b[l].reshape(kvl, a_heads, A_NOPE + av)
        w_ukt = wkv3[:, :, :A_NOPE].transpose(1, 2, 0).astype(BF16)
        w_uv = wkv3[:, :, A_NOPE:].transpose(1, 0, 2).astype(BF16)
        w_o1 = w_out[l][:wv].astype(BF16)
        w_o2 = w_out[l][wv:].astype(BF16)
        w_rg = jnp.concatenate(
            [w_group[l], w_router[l], jnp.zeros((d, LANES - n_groups - n_exp), F32)], axis=1)
        b_rg = jnp.concatenate(
            [b_group[l], b_router[l], jnp.zeros((LANES - n_groups - n_exp,), F32)]).reshape(1, LANES)
        w_pg = w_ple_gate[l].astype(BF16)
        w_pp = w_ple_proj[l].astype(BF16)
        g_mix = norm_mix[l].reshape(1, d)

        zz = _in_proj(hp, g_mix, w_main, w_small, m, 0, None)
        z, zs = _in_proj(hs, g_mix, w_main, w_small, m, mp, zz)
        q_hm, ckv, kr = _mla_q(z, zs, cos_t, sin_t, q_a_norm[l].reshape(1, ql), kv_a_norm[l].reshape(1, kvl),
                               wq_re, a_heads, ql, kvl, rope, 2 * wq + 2 * wv, 2 * wq + 2 * wv + ql)
        ckv_p, ckv_s, kr_p, kr_s = ckv[:mp], ckv[mp:], kr[:mp], kr[mp:]

        norm_m = mlstm_norm[l].reshape(1, wv)
        lp = _pick(seq, 256, 16)
        hm, c_p, n_p, m_p = _mlstm(z, zs, gate_bias, norm_m, bp, seq, lp, 1, 0, hm_heads, dk, dv, m, None, None)
        n_sub = 2 if n_dec % 2 == 0 else 1
        hm, c_s, n_s, m_s = _mlstm(z, zs, gate_bias, norm_m, n_dec, ts, ts, n_sub, mp, hm_heads, dk, dv, m,
                                   (state_C[l], state_n[l], state_m[l].reshape(n_dec, 1, hm_heads)), hm)

        k_hm, v_hm = _kv_expand(ckv_p, kr_p, wkvb, mp, a_heads, av)
        att = _flash_prompt(q_hm, k_hm, v_hm, bp, seq, m)
        qs = _q_latent(q_hm, w_ukt, mp, n_dec, ts)
        o_s = _mla_sample(page_table, qs, ckv_s, kr_s, cache_ckv[l], cache_krope[l], (A_NOPE + rope) ** -0.5)
        att = _o_uv(o_s, w_uv, att, mp)

        h1 = _out_proj(hm, att, w_o1, w_o2, hp, 0, m, None)
        h1 = _out_proj(hm, att, w_o1, w_o2, hs, mp, m, h1)

        xn2p, logits = _router(h1, norm_ffn[l].reshape(1, d), w_rg, b_rg)
        row_src, posn, wts, blk_expert, n_used, n_used_tiles, n_valid = _route(
            logits, n_groups, epg, moe_tm, moe_tile)
        x_sorted = _dispatch(xn2p, row_src, n_used_tiles, n_valid, moe_tile, d)
        yp = _moe_ffn(x_sorted, blk_expert, n_used, w_gate[l], w_up[l], w_down[l], moe_tm)
        h2, h2b = _combine(h1, yp, posn, wts)

        last = l == depth - 1
        g_fin = norm_final.reshape(1, d)
        hp = _ple(h2b, h2, p_prompt[l].reshape(mp, -1), w_pg, w_pp, g_fin, 0, mp, last)
        hs = _ple(h2b, h2, p_sample[l].reshape(ms, -1), w_pg, w_pp, g_fin, mp, ms, last)

        outs["ckv_p"].append(ckv_p.reshape(bp, seq, kvl))
        outs["kr_p"].append(kr_p.reshape(bp, seq, rope))
        outs["C_p"].append(c_p)
        outs["n_p"].append(n_p)
        outs["m_p"].append(m_p.reshape(bp, hm_heads))
        outs["ckv_s"].append(ckv_s.reshape(n_dec, ts, kvl))
        outs["kr_s"].append(kr_s.reshape(n_dec, ts, rope))
        outs["C_s"].append(c_s)
        outs["n_s"].append(n_s)
        outs["m_s"].append(m_s.reshape(n_dec, hm_heads))

    st = {k: jnp.stack(v) for k, v in outs.items()}
    return (hp.reshape(bp, seq, d), hs.reshape(n_dec, ts, d),
            st["ckv_p"], st["kr_p"], st["C_p"], st["n_p"], st["m_p"],
            st["ckv_s"], st["kr_s"], st["C_s"], st["n_s"], st["m_s"])
```

```python
import functools
import math

import jax
import jax.numpy as jnp
from jax import lax
from jax.experimental import pallas as pl
from jax.experimental.pallas import tpu as pltpu

F32 = jnp.float32
BF16 = jnp.bfloat16
SDS = jax.ShapeDtypeStruct

EPS = 1e-6
GATE_CAP = 15.0
ROPE_THETA = 10000.0
A_NOPE = 128
TOP_K = 2

LANES = 128
SUBLANES = 8
VMEM_LIMIT = 56 * 1024 * 1024
NEG_INF = float("-inf")

NT_DIMS = (((1,), (1,)), ((), ()))


def _cparams(sem, vmem=VMEM_LIMIT):
    return pltpu.CompilerParams(dimension_semantics=sem, vmem_limit_bytes=vmem)


def _pick(n, pref, mult):
    t = min(pref, n) // mult * mult
    while t >= mult:
        if n % t == 0:
            return t
        t -= mult
    return n


def _rup(x, m):
    return (x + m - 1) // m * m


def _dot(a, b):
    return jnp.dot(a, b, preferred_element_type=F32)


def _dot_nt(a, b, precision=None):
    return lax.dot_general(a, b, NT_DIMS, preferred_element_type=F32, precision=precision)


def _rms(x, g):
    ms = jnp.mean(x * x, axis=-1, keepdims=True)
    return x * lax.rsqrt(ms + EPS) * g


def _inproj_kernel(x_ref, g_ref, w_ref, ws_ref, *rest):
    z_ref, zs_ref, xn_ref = rest[-3:]
    j = pl.program_id(1)

    @pl.when(j == 0)
    def _():
        xn = _rms(x_ref[...], g_ref[...]).astype(BF16)
        xn_ref[...] = xn
        zs_ref[...] = _dot(xn, ws_ref[...])

    z_ref[...] = _dot(xn_ref[...], w_ref[...])


def _in_proj(x, g, w_main, w_small, m_total, row0, prev):
    rows, d = x.shape
    nm = w_main.shape[1]
    tm = _pick(math.gcd(rows, row0) if row0 else rows, 512, 16)
    tn = _pick(nm, 512, LANES)
    ro = row0 // tm
    ins = [x, g, w_main, w_small]
    in_specs = [
        pl.BlockSpec((tm, d), lambda i, j: (i, 0)),
        pl.BlockSpec((1, d), lambda i, j: (0, 0)),
        pl.BlockSpec((d, tn), lambda i, j: (0, j)),
        pl.BlockSpec((d, LANES), lambda i, j: (0, 0)),
    ]
    aliases = {}
    if prev is not None:
        ins += list(prev)
        in_specs += [pl.BlockSpec(memory_space=pl.ANY)] * 2
        aliases = {4: 0, 5: 1}
    return pl.pallas_call(
        _inproj_kernel,
        grid=(rows // tm, nm // tn),
        in_specs=in_specs,
        out_specs=[pl.BlockSpec((tm, tn), lambda i, j: (i + ro, j)),
                   pl.BlockSpec((tm, LANES), lambda i, j: (i + ro, 0))],
        out_shape=[SDS((m_total, nm), F32), SDS((m_total, LANES), F32)],
        scratch_shapes=[pltpu.VMEM((tm, d), BF16)],
        input_output_aliases=aliases,
        compiler_params=_cparams(("arbitrary", "arbitrary")),
        name="in_proj",
    )(*ins)


def _rope_lanes(y, cos, sin_signed, width):
    half = 32
    lane = lax.broadcasted_iota(jnp.int32, y.shape, 1)
    first = (lane % (2 * half)) < half
    swapped = jnp.where(first, pltpu.roll(y, width - half, 1), pltpu.roll(y, half, 1))
    return y * cos + swapped * sin_signed


def _mlaq_kernel(n_heads, rope, aq_ref, c_ref, zs_ref, cos_ref, sin_ref, qg_ref, kvg_ref, wq_ref,
                 q_ref, ckv_ref, kr_ref):
    aqn = _rms(aq_ref[...], qg_ref[...]).astype(BF16)
    y = _dot(aqn, wq_ref[...])
    cos = cos_ref[...]
    sin = sin_ref[...]
    nope_w = n_heads * A_NOPE
    for h2 in range(n_heads // 2):
        yr = y[:, nope_w + h2 * LANES: nope_w + (h2 + 1) * LANES]
        rot = _rope_lanes(yr, cos, sin, LANES)
        for s in range(2):
            h = 2 * h2 + s
            q_ref[h, :, 0:A_NOPE] = y[:, h * A_NOPE:(h + 1) * A_NOPE].astype(BF16)
            q_ref[h, :, A_NOPE:A_NOPE + rope] = rot[:, s * rope:(s + 1) * rope].astype(BF16)
    ckv_ref[...] = _rms(c_ref[...], kvg_ref[...])
    kr = _rope_lanes(zs_ref[...], cos, sin, LANES)
    kr_ref[...] = kr[:, 0:rope]


def _mla_q(z, zs, cos, sin, qg, kvg, wq, n_heads, ql, kvl, rope, aq_off, c_off):
    m = z.shape[0]
    tm = _pick(m, 256, 16)
    qd = A_NOPE + rope
    return pl.pallas_call(
        functools.partial(_mlaq_kernel, n_heads, rope),
        grid=(m // tm,),
        in_specs=[
            pl.BlockSpec((tm, ql), lambda i: (i, aq_off // ql)),
            pl.BlockSpec((tm, kvl), lambda i: (i, c_off // kvl)),
            pl.BlockSpec((tm, LANES), lambda i: (i, 0)),
            pl.BlockSpec((tm, LANES), lambda i: (i, 0)),
            pl.BlockSpec((tm, LANES), lambda i: (i, 0)),
            pl.BlockSpec((1, ql), lambda i: (0, 0)),
            pl.BlockSpec((1, kvl), lambda i: (0, 0)),
            pl.BlockSpec((ql, n_heads * qd), lambda i: (0, 0)),
        ],
        out_specs=[
            pl.BlockSpec((n_heads, tm, qd), lambda i: (0, i, 0)),
            pl.BlockSpec((tm, kvl), lambda i: (i, 0)),
            pl.BlockSpec((tm, rope), lambda i: (i, 0)),
        ],
        out_shape=[SDS((n_heads, m, qd), BF16), SDS((m, kvl), F32), SDS((m, rope), F32)],
        compiler_params=_cparams(("arbitrary",)),
        name="mla_q",
    )(z, z, zs, cos, sin, qg, kvg, wq)


def _kv_kernel(n_heads, av, rope, c_ref, kr_ref, w_ref, k_ref, v_ref):
    y = _dot(c_ref[...].astype(BF16), w_ref[...])
    kr = kr_ref[...].astype(BF16)
    hw = A_NOPE + av
    for h in range(n_heads):
        k_ref[h, :, 0:A_NOPE] = y[:, h * hw: h * hw + A_NOPE].astype(BF16)
        k_ref[h, :, A_NOPE:A_NOPE + rope] = kr
        v_ref[h] = y[:, h * hw + A_NOPE:(h + 1) * hw].astype(BF16)


def _kv_expand(ckv, kr, w_kvb, rows, n_heads, av):
    kvl = ckv.shape[1]
    rope = kr.shape[1]
    tm = _pick(rows, 256, 16)
    return pl.pallas_call(
        functools.partial(_kv_kernel, n_heads, av, rope),
        grid=(rows // tm,),
        in_specs=[
            pl.BlockSpec((tm, kvl), lambda i: (i, 0)),
            pl.BlockSpec((tm, rope), lambda i: (i, 0)),
            pl.BlockSpec(w_kvb.shape, lambda i: (0, 0)),
        ],
        out_specs=[
            pl.BlockSpec((n_heads, tm, A_NOPE + rope), lambda i: (0, i, 0)),
            pl.BlockSpec((n_heads, tm, av), lambda i: (0, i, 0)),
        ],
        out_shape=[SDS((n_heads, rows, A_NOPE + rope), BF16), SDS((n_heads, rows, av), BF16)],
        compiler_params=_cparams(("arbitrary",)),
        name="kv_expand",
    )(ckv, kr, w_kvb)


def _flash_kernel(tq, tk, scale, q_ref, k_ref, v_ref, o_ref):
    qi = pl.program_id(2)
    q = q_ref[0]
    av = v_ref.shape[-1]
    row = qi * tq + lax.broadcasted_iota(jnp.int32, (tq, tk), 0)
    col0 = lax.broadcasted_iota(jnp.int32, (tq, tk), 1)

    def body(masked, kb, carry):
        m, l, acc = carry
        start = pl.multiple_of(kb * tk, tk)
        k = k_ref[0, pl.ds(start, tk), :]
        v = v_ref[0, pl.ds(start, tk), :]
        s = _dot_nt(q, k) * scale
        if masked:
            s = jnp.where(col0 + kb * tk <= row, s, NEG_INF)
        m_new = jnp.maximum(m, jnp.max(s, axis=-1, keepdims=True))
        p = jnp.exp(s - m_new)
        alpha = jnp.exp(m - m_new)
        l = alpha * l + jnp.sum(p, axis=-1, keepdims=True)
        acc = alpha * acc + _dot(p.astype(BF16), v)
        return m_new, l, acc

    n_full = (qi * tq) // tk
    n_kb = (qi * tq + tq + tk - 1) // tk
    init = (jnp.full((tq, 1), NEG_INF, F32), jnp.zeros((tq, 1), F32), jnp.zeros((tq, av), F32))
    carry = lax.fori_loop(0, n_full, functools.partial(body, False), init)
    _, l, acc = lax.fori_loop(n_full, n_kb, functools.partial(body, True), carry)
    o_ref[...] = (acc / l).astype(o_ref.dtype)


def _flash_prompt(q_hm, k_hm, v_hm, n_batch, seq, m_total):
    n_heads, _, qd = q_hm.shape
    av = v_hm.shape[-1]
    tq = _pick(seq, 512, 16)
    tk = _pick(seq, 512, 16)
    nq = seq // tq
    return pl.pallas_call(
        functools.partial(_flash_kernel, tq, tk, qd ** -0.5),
        grid=(n_batch, n_heads, nq),
        in_specs=[
            pl.BlockSpec((1, tq, qd), lambda b, h, i: (h, b * nq + i, 0)),
            pl.BlockSpec((1, seq, qd), lambda b, h, i: (h, b, 0)),
            pl.BlockSpec((1, seq, av), lambda b, h, i: (h, b, 0)),
        ],
        out_specs=pl.BlockSpec((tq, av), lambda b, h, i: (b * nq + i, h)),
        out_shape=SDS((m_total, n_heads * av), BF16),
        compiler_params=_cparams(("arbitrary", "arbitrary", "arbitrary")),
        name="flash_prompt",
    )(q_hm, k_hm, v_hm)


def _qlat_kernel(n_seq, ts, kvl, rope, q_ref, w_ref, o_ref):
    q = q_ref[0]
    qlat = _dot(q[:, 0:A_NOPE], w_ref[0])
    width = o_ref.shape[-1]
    full = jnp.concatenate(
        [qlat, q[:, A_NOPE:A_NOPE + rope].astype(F32),
         jnp.zeros((q.shape[0], width - kvl - rope), F32)], axis=1)
    o_ref[...] = full.reshape(n_seq, 1, ts, width)


def _q_latent(q_hm, w_ukt, rows0, n_seq, ts):
    n_heads, _, qd = q_hm.shape
    kvl = w_ukt.shape[-1]
    rope = qd - A_NOPE
    ms = n_seq * ts
    width = _rup(kvl + rope, LANES)
    return pl.pallas_call(
        functools.partial(_qlat_kernel, n_seq, ts, kvl, rope),
        grid=(n_heads,),
        in_specs=[
            pl.BlockSpec((1, ms, qd), lambda h: (h, rows0 // ms, 0)),
            pl.BlockSpec((1, A_NOPE, kvl), lambda h: (h, 0, 0)),
        ],
        out_specs=pl.BlockSpec((n_seq, 1, ts, width), lambda h: (0, h, 0, 0)),
        out_shape=SDS((n_seq, n_heads, ts, width), F32),
        compiler_params=_cparams(("arbitrary",)),
        name="q_latent",
    )(q_hm, w_ukt)


MLA_BUFFERS = 3
MLA_STREAMS = 4


def _mla_sample_kernel(n_pages, chunk_pages, page, n_seq, ts, kvl, rope, scale,
                       pt_ref, q_ref, cn_ref, kn_ref, ckv_hbm, krt_hbm, o_ref,
                       kbuf, rbuf, sem, *stats):
    m_ref, l_ref, acc_ref = stats[0::3], stats[1::3], stats[2::3]
    b = pl.program_id(0)
    n_chunks = n_pages // chunk_pages
    n_total = n_seq * n_chunks
    rows = q_ref.shape[1] * ts
    keys = chunk_pages * page
    part = keys // MLA_STREAMS

    def chunk_copies(g, slot):
        gw = g % n_total
        base = (gw // n_chunks) * n_pages + (gw % n_chunks) * chunk_pages
        copies = []
        for p in range(chunk_pages):
            pg = pt_ref[base + p]
            copies.append(pltpu.make_async_copy(
                ckv_hbm.at[pg], kbuf.at[slot, pl.ds(p * page, page)], sem.at[slot]))
            copies.append(pltpu.make_async_copy(
                krt_hbm.at[pg], rbuf.at[slot, :, pl.ds(p * page, page)], sem.at[slot]))
        return copies

    @pl.when(b == 0)
    def _():
        for g0 in range(MLA_BUFFERS - 1):
            for cp in chunk_copies(g0, g0):
                cp.start()

    q = q_ref[0].reshape(rows, q_ref.shape[-1])
    ql = q[:, 0:kvl].astype(BF16)
    qr = q[:, kvl:kvl + rope].astype(BF16)
    for j in range(MLA_STREAMS):
        m_ref[j][...] = jnp.full(m_ref[j].shape, NEG_INF, F32)
        l_ref[j][...] = jnp.zeros(l_ref[j].shape, F32)
        acc_ref[j][...] = jnp.zeros(acc_ref[j].shape, F32)

    def softmax_step(j, s):
        m = m_ref[j][...]
        m_new = jnp.maximum(m, jnp.max(s, axis=-1, keepdims=True))
        p = jnp.exp(s - m_new)
        alpha = jnp.exp(m - m_new)
        l_ref[j][...] = alpha * l_ref[j][...] + jnp.sum(p, axis=-1, keepdims=True)
        m_ref[j][...] = m_new
        return p.astype(BF16), alpha

    def accumulate(j, p, alpha, values):
        acc_ref[j][...] = alpha * acc_ref[j][...] + _dot(p, values)

    def body(c, carry):
        g = b * n_chunks + c
        slot = g % MLA_BUFFERS
        ahead = g + MLA_BUFFERS - 1
        for cp in chunk_copies(ahead, ahead % MLA_BUFFERS):
            cp.start()
        for cp in chunk_copies(g, slot):
            cp.wait()
        ks = [kbuf[slot, j * part:(j + 1) * part, :].astype(BF16) for j in range(MLA_STREAMS)]
        ss = [(_dot_nt(ql, ks[j]) + _dot(qr, rbuf[slot, :, j * part:(j + 1) * part].astype(BF16))) * scale
              for j in range(MLA_STREAMS)]
        pa = [softmax_step(j, ss[j]) for j in range(MLA_STREAMS)]
        for j in range(MLA_STREAMS):
            accumulate(j, pa[j][0], pa[j][1], ks[j])
        return carry

    lax.fori_loop(0, n_chunks, body, 0)

    @pl.when(b == n_seq - 1)
    def _():
        for d in range(MLA_BUFFERS - 1):
            for cp in chunk_copies(n_total + d, (n_total + d) % MLA_BUFFERS):
                cp.wait()

    pad = 2 * SUBLANES - ts
    cn = jnp.concatenate([cn_ref[...], jnp.zeros((pad, kvl), F32)], axis=0).astype(BF16)
    kn = jnp.concatenate([kn_ref[...], jnp.zeros((pad, rope), F32)], axis=0).astype(BF16)
    s = (_dot_nt(ql, cn) + _dot_nt(qr, kn)) * scale
    tok = lax.broadcasted_iota(jnp.int32, s.shape, 0) % ts
    col = lax.broadcasted_iota(jnp.int32, s.shape, 1)
    s = jnp.where(col <= tok, s, NEG_INF)
    p_new, alpha_new = softmax_step(0, s)
    accumulate(0, p_new, alpha_new, cn)

    m_all = m_ref[0][...]
    for j in range(1, MLA_STREAMS):
        m_all = jnp.maximum(m_all, m_ref[j][...])
    l_all = jnp.zeros(m_all.shape, F32)
    acc = jnp.zeros(acc_ref[0].shape, F32)
    for j in range(MLA_STREAMS):
        w = jnp.exp(m_ref[j][...] - m_all)
        l_all = l_all + w * l_ref[j][...]
        acc = acc + w * acc_ref[j][...]
    o_ref[0] = (acc / l_all).reshape(o_ref.shape[1:])


def _mla_sample(page_table, qs, ckv_s, kr_s, cache_ckv, cache_krt, scale):
    n_seq, n_heads, ts, width = qs.shape
    n_pages = page_table.shape[1]
    page, kvl = cache_ckv.shape[1:]
    rope = cache_krt.shape[1]
    chunk_pages = _pick(n_pages, 8, 1)
    rows = n_heads * ts
    keys = chunk_pages * page
    assert n_seq * (n_pages // chunk_pages) >= MLA_BUFFERS and keys % (MLA_STREAMS * LANES) == 0
    kern = functools.partial(_mla_sample_kernel, n_pages, chunk_pages, page, n_seq, ts, kvl, rope, scale)
    return pl.pallas_call(
        kern,
        grid_spec=pltpu.PrefetchScalarGridSpec(
            num_scalar_prefetch=1,
            grid=(n_seq,),
            in_specs=[
                pl.BlockSpec((1, n_heads, ts, width), lambda b, pt: (b, 0, 0, 0)),
                pl.BlockSpec((ts, kvl), lambda b, pt: (b, 0)),
                pl.BlockSpec((ts, rope), lambda b, pt: (b, 0)),
                pl.BlockSpec(memory_space=pl.ANY),
                pl.BlockSpec(memory_space=pl.ANY),
            ],
            out_specs=pl.BlockSpec((1, n_heads, ts, kvl), lambda b, pt: (b, 0, 0, 0)),
            scratch_shapes=[
                pltpu.VMEM((MLA_BUFFERS, keys, kvl), F32),
                pltpu.VMEM((MLA_BUFFERS, rope, keys), F32),
                pltpu.SemaphoreType.DMA((MLA_BUFFERS,)),
            ] + MLA_STREAMS * [pltpu.VMEM((rows, 1), F32), pltpu.VMEM((rows, 1), F32),
                               pltpu.VMEM((rows, kvl), F32)],
        ),
        out_shape=SDS((n_seq, n_heads, ts, kvl), F32),
        compiler_params=_cparams(("arbitrary",)),
        name="mla_sample",
    )(page_table.reshape(-1), qs, ckv_s, kr_s, cache_ckv, cache_krt)


def _ouv_kernel(o_ref, w_ref, prev_ref, att_ref):
    del prev_ref
    o = o_ref[...]
    o = o.reshape(o.shape[0] * o.shape[2], o.shape[3]).astype(BF16)
    att_ref[...] = _dot(o, w_ref[0]).astype(att_ref.dtype)


def _o_uv(o_s, w_uv, att, rows0):
    n_seq, n_heads, ts, kvl = o_s.shape
    av = w_uv.shape[-1]
    ms = n_seq * ts
    return pl.pallas_call(
        _ouv_kernel,
        grid=(n_heads,),
        in_specs=[
            pl.BlockSpec((n_seq, 1, ts, kvl), lambda h: (0, h, 0, 0)),
            pl.BlockSpec((1, kvl, av), lambda h: (h, 0, 0)),
            pl.BlockSpec(memory_space=pl.ANY),
        ],
        out_specs=pl.BlockSpec((ms, av), lambda h: (rows0 // ms, h)),
        out_shape=SDS(att.shape, att.dtype),
        input_output_aliases={2: 0},
        compiler_params=_cparams(("arbitrary",)),
        name="o_uv",
    )(o_s, w_uv, att)


def _mlstm_kernel(n_sub, L, n_heads, dk, dv, has_state, *refs):
    if has_state:
        (q_ref, k_ref, v_ref, mo_ref, zs_ref, bias_ref, norm_ref, c0_ref, n0_ref, m0_ref, _prev,
         hm_ref, c_ref, n_ref, m_ref) = refs
    else:
        (q_ref, k_ref, v_ref, mo_ref, zs_ref, bias_ref, norm_ref,
         hm_ref, c_ref, n_ref, m_ref) = refs
    ci = pl.program_id(1)
    Lk = _rup(L, LANES)

    @pl.when(ci == 0)
    def _():
        if has_state:
            c_ref[...] = c0_ref[...]
            n_ref[...] = n0_ref[...]
            m_ref[...] = m0_ref[...]
        else:
            c_ref[...] = jnp.zeros(c_ref.shape, F32)
            n_ref[...] = jnp.zeros(n_ref.shape, F32)
            m_ref[...] = jnp.zeros(m_ref.shape, F32)

    def pad_rows(a):
        if Lk == L:
            return a
        return jnp.concatenate([a, jnp.zeros((Lk - L, a.shape[1]), a.dtype)], axis=0)

    row = lax.broadcasted_iota(jnp.int32, (L, Lk), 0)
    col = lax.broadcasted_iota(jnp.int32, (L, Lk), 1)
    causal = col <= row
    tril = causal.astype(F32)
    sel_r = lax.broadcasted_iota(jnp.int32, (SUBLANES, LANES), 0)
    sel_c = lax.broadcasted_iota(jnp.int32, (SUBLANES, LANES), 1)
    lane0 = LANES // 2
    sel = ((sel_c == sel_r + lane0) & (sel_r < n_heads)).astype(F32)
    eye = (lax.broadcasted_iota(jnp.int32, (dk, dk), 0)
           == lax.broadcasted_iota(jnp.int32, (dk, dk), 1)).astype(BF16)
    hi = lax.Precision.HIGHEST

    for sq in range(n_sub):
        rs = slice(sq * L, (sq + 1) * L)
        pre = zs_ref[rs, :] + bias_ref[...]
        cap = GATE_CAP * jnp.tanh(pre / GATE_CAP)
        i_all = cap
        f_all = jnp.minimum(cap, 0.0) - jnp.log1p(jnp.exp(-jnp.abs(cap)))
        bcum_all = jnp.dot(tril, pad_rows(f_all), preferred_element_type=F32, precision=hi)
        r_all = i_all - pltpu.roll(bcum_all, LANES - n_heads, 1)
        r_rows = _dot_nt(sel, pad_rows(r_all), precision=hi)
        for h in range(n_heads):
            i_col = i_all[:, lane0 + h: lane0 + h + 1]
            b_col = bcum_all[:, lane0 + n_heads + h: lane0 + n_heads + h + 1]
            r_row = r_rows[h:h + 1, :]
            b_tot = b_col[L - 1:L, :]
            m_prev = m_ref[sq, :, h:h + 1]
            c_prev = c_ref[sq, h]
            n_prev = n_ref[sq, h:h + 1, :]

            q = q_ref[rs, h * dk:(h + 1) * dk] * (dk ** -0.5)
            k = k_ref[rs, h * dk:(h + 1) * dk]
            v = v_ref[rs, h * dv:(h + 1) * dv]
            qb = q.astype(BF16)
            kpb = pad_rows(k).astype(BF16)
            vpb = pad_rows(v).astype(BF16)

            dlog = jnp.where(causal, b_col + r_row, NEG_INF)
            inter = b_col + m_prev
            m_t = jnp.maximum(inter, jnp.max(dlog, axis=-1, keepdims=True))
            s = _dot_nt(qb, kpb) * jnp.exp(dlog - m_t)
            w_prev = jnp.exp(inter - m_t)
            num = _dot(s.astype(BF16), vpb) + w_prev * _dot(qb, c_prev.astype(BF16))
            den = (jnp.sum(s, axis=-1, keepdims=True)
                   + w_prev * jnp.sum(q * n_prev, axis=-1, keepdims=True))
            hout = num / jnp.maximum(jnp.abs(den), jnp.exp(-m_t))

            g = b_tot - b_col + i_col
            m_new = jnp.maximum(b_tot + m_prev, jnp.max(g, axis=0, keepdims=True))
            wk = jnp.exp(g - m_new)
            decay = jnp.exp(b_tot + m_prev - m_new)
            kw = wk * k
            kw_t = _dot_nt(eye, pad_rows(kw).astype(BF16)).astype(BF16)
            c_ref[sq, h] = decay * c_prev + _dot(kw_t, vpb)
            n_ref[sq, h:h + 1, :] = decay * n_prev + jnp.sum(kw, axis=0, keepdims=True)
            m_ref[sq, :, h:h + 1] = m_new

            hn = _rms(hout, norm_ref[:, h * dv:(h + 1) * dv])
            hn = hn * jax.nn.sigmoid(mo_ref[rs, h * dv:(h + 1) * dv])
            hm_ref[rs, h * dv:(h + 1) * dv] = hn.astype(hm_ref.dtype)


def _mlstm(z, zs, bias, norm, n_seq, seq, L, n_sub, rows0, n_heads, dk, dv, m_total, state, prev_hm):
    wq = n_heads * dk
    wv = n_heads * dv
    n_chunks = seq // L
    rb = n_sub * L
    has_state = state is not None
    if has_state:
        assert n_chunks == 1
    ro = rows0 // rb

    def rmap(cb):
        return lambda b, c: (ro + b * n_chunks + c, cb)

    ins = [z, z, z, z, zs, bias, norm]
    in_specs = [
        pl.BlockSpec((rb, wq), rmap(0)),
        pl.BlockSpec((rb, wq), rmap(1)),
        pl.BlockSpec((rb, wv), rmap(1)),
        pl.BlockSpec((rb, wv), rmap(2)),
        pl.BlockSpec((rb, LANES), rmap(0)),
        pl.BlockSpec((1, LANES), lambda b, c: (0, 0)),
        pl.BlockSpec((1, wv), lambda b, c: (0, 0)),
    ]
    c_spec = pl.BlockSpec((n_sub, n_heads, dk, dv), lambda b, c: (b, 0, 0, 0))
    n_spec = pl.BlockSpec((n_sub, n_heads, dk), lambda b, c: (b, 0, 0))
    m_spec = pl.BlockSpec((n_sub, 1, n_heads), lambda b, c: (b, 0, 0))
    aliases = {}
    if has_state:
        ins += [state[0], state[1], state[2], prev_hm]
        in_specs += [c_spec, n_spec, m_spec, pl.BlockSpec(memory_space=pl.ANY)]
        aliases = {10: 0}
    return pl.pallas_call(
        functools.partial(_mlstm_kernel, n_sub, L, n_heads, dk, dv, has_state),
        grid=(n_seq // n_sub, n_chunks),
        in_specs=in_specs,
        out_specs=[pl.BlockSpec((rb, wv), rmap(0)), c_spec, n_spec, m_spec],
        out_shape=[SDS((m_total, wv), BF16), SDS((n_seq, n_heads, dk, dv), F32),
                   SDS((n_seq, n_heads, dk), F32), SDS((n_seq, 1, n_heads), F32)],
        input_output_aliases=aliases,
        compiler_params=_cparams(("arbitrary", "arbitrary")),
        name="mlstm_state" if has_state else "mlstm_prompt",
    )(*ins)


def _outproj_kernel(hm_ref, att_ref, w1_ref, w2_ref, x_ref, *rest):
    o_ref = rest[-1]
    o_ref[...] = x_ref[...] + _dot(hm_ref[...], w1_ref[...]) + _dot(att_ref[...], w2_ref[...])


def _out_proj(hm, att, w1, w2, x, rows0, m_total, prev):
    rows, d = x.shape
    k1, k2 = hm.shape[1], att.shape[1]
    tm = _pick(math.gcd(rows, rows0) if rows0 else rows, 512, 16)
    tn = _pick(d, 512, LANES)
    ro = rows0 // tm
    ins = [hm, att, w1, w2, x]
    in_specs = [
        pl.BlockSpec((tm, k1), lambda i, j: (i + ro, 0)),
        pl.BlockSpec((tm, k2), lambda i, j: (i + ro, 0)),
        pl.BlockSpec((k1, tn), lambda i, j: (0, j)),
        pl.BlockSpec((k2, tn), lambda i, j: (0, j)),
        pl.BlockSpec((tm, tn), lambda i, j: (i, j)),
    ]
    aliases = {}
    if prev is not None:
        ins.append(prev)
        in_specs.append(pl.BlockSpec(memory_space=pl.ANY))
        aliases = {5: 0}
    return pl.pallas_call(
        _outproj_kernel,
        grid=(rows // tm, d // tn),
        in_specs=in_specs,
        out_specs=pl.BlockSpec((tm, tn), lambda i, j: (i + ro, j)),
        out_shape=SDS((m_total, d), F32),
        input_output_aliases=aliases,
        compiler_params=_cparams(("arbitrary", "arbitrary")),
        name="out_proj",
    )(*ins)


MOE_BLOCK = 512
MOE_SUB = 256
MOE_TILE = 256


def _row_pitch(d):
    return d // LANES + SUBLANES


def _store_token_rows(ref2d, val, row0, pitch):
    tm = val.shape[0]
    for cc in range(val.shape[1] // LANES):
        ref2d[pl.ds(row0 + cc, tm, stride=pitch), :] = val[:, cc * LANES:(cc + 1) * LANES]


def _router_kernel(pitch, h_ref, g_ref, w_ref, b_ref, xp_ref, lg_ref):
    xn = _rms(h_ref[...], g_ref[...])
    _store_token_rows(xp_ref, xn, 0, pitch)
    lg_ref[...] = jnp.dot(xn, w_ref[...], preferred_element_type=F32,
                          precision=lax.Precision.HIGHEST) + b_ref[...]


def _router(h, g, w_rg, b_rg):
    m, d = h.shape
    tm = _pick(m, 256, 16)
    pitch = _row_pitch(d)
    return pl.pallas_call(
        functools.partial(_router_kernel, pitch),
        grid=(m // tm,),
        in_specs=[
            pl.BlockSpec((tm, d), lambda i: (i, 0)),
            pl.BlockSpec((1, d), lambda i: (0, 0)),
            pl.BlockSpec((d, LANES), lambda i: (0, 0)),
            pl.BlockSpec((1, LANES), lambda i: (0, 0)),
        ],
        out_specs=[pl.BlockSpec((tm * pitch, LANES), lambda i: (i, 0)),
                   pl.BlockSpec((tm, LANES), lambda i: (i, 0))],
        out_shape=[SDS((m * pitch, LANES), F32), SDS((m, LANES), F32)],
        compiler_params=_cparams(("arbitrary",)),
        name="ffn_router",
    )(h, g, w_rg, b_rg)


def _dispatch_kernel(tile, pitch, n_chunks, nu_ref, nv_ref, idx_ref, xp_hbm, o_ref, stage, sem):
    i = pl.program_id(0)
    nu = nu_ref[0]
    slot_rows = tile * pitch

    def copy(t, r, slot):
        src = pl.multiple_of(idx_ref[t * tile + r] * pitch, SUBLANES)
        dst = pl.multiple_of(slot * slot_rows + r * pitch, SUBLANES)
        return pltpu.make_async_copy(
            xp_hbm.at[pl.ds(src, n_chunks)], stage.at[pl.ds(dst, n_chunks)], sem.at[slot])

    def start_tile(t, slot):
        def start(r, c):
            copy(t, r, slot).start()
            return c
        lax.fori_loop(0, nv_ref[t], start, 0)

    @pl.when(i == 0)
    def _():
        stage[...] = jnp.zeros(stage.shape, F32)
        start_tile(0, 0)

    @pl.when(i < nu)
    def _():
        slot = i % 2

        @pl.when(i + 1 < nu)
        def _():
            start_tile(i + 1, 1 - slot)

        def wait(r, c):
            copy(i, r, slot).wait()
            return c
        lax.fori_loop(0, nv_ref[i], wait, 0)
        for c in range(n_chunks):
            v = stage[pl.ds(slot * slot_rows + c, tile, stride=pitch), :]
            o_ref[:, c * LANES:(c + 1) * LANES] = v.astype(BF16)


def _dispatch(xp, idx, n_used_tiles, n_valid, tile, d):
    n = idx.shape[0]
    nch = d // LANES
    pitch = _row_pitch(d)

    def blk(i, nu, nv, ix):
        return (jnp.minimum(i, nu[0] - 1), 0)

    return pl.pallas_call(
        functools.partial(_dispatch_kernel, tile, pitch, nch),
        grid_spec=pltpu.PrefetchScalarGridSpec(
            num_scalar_prefetch=3,
            grid=(n // tile,),
            in_specs=[pl.BlockSpec(memory_space=pl.ANY)],
            out_specs=pl.BlockSpec((tile, d), blk),
            scratch_shapes=[pltpu.VMEM((2 * tile * pitch, LANES), F32), pltpu.SemaphoreType.DMA((2,))],
        ),
        out_shape=SDS((n, d), BF16),
        compiler_params=_cparams(("arbitrary",)),
        name="moe_dispatch",
    )(n_used_tiles, n_valid, idx, xp)


def _moe_up_kernel(sub, be_ref, nu_ref, nvb_ref, x_ref, wg_ref, wu_ref, hid_ref, wgu_ref):
    b = pl.program_id(0)

    @pl.when(b < nu_ref[0])
    def _():
        tf = wg_ref.shape[-1]
        wgu_ref[:, 0:tf] = wg_ref[0].astype(BF16)
        wgu_ref[:, tf:2 * tf] = wu_ref[0].astype(BF16)

        def sub_block(s, c):
            rs = pl.ds(pl.multiple_of(s * sub, sub), sub)
            gu = _dot(x_ref[rs, :], wgu_ref[...])
            hid_ref[rs, :] = (jax.nn.silu(gu[:, 0:tf]) * gu[:, tf:2 * tf]).astype(BF16)
            return c

        lax.fori_loop(0, (nvb_ref[b] + sub - 1) // sub, sub_block, 0)


def _moe_down_kernel(sub, pitch, be_ref, nu_ref, nvb_ref, hid_ref, wd_ref, yp_ref, wd_bf_ref):
    b = pl.program_id(0)
    j = pl.program_id(1)

    @pl.when(b < nu_ref[0])
    def _():
        tn = wd_ref.shape[-1]
        wd_bf_ref[...] = wd_ref[0].astype(BF16)

        def sub_block(s, c):
            r0 = pl.multiple_of(s * sub, sub)
            y = _dot(hid_ref[pl.ds(r0, sub), :], wd_bf_ref[...])
            _store_token_rows(yp_ref, y, r0 * pitch + j * (tn // LANES), pitch)
            return c

        lax.fori_loop(0, (nvb_ref[b] + sub - 1) // sub, sub_block, 0)


def _moe_ffn(x_sorted, blk_expert, n_used, n_valid_blk, w_gate, w_up, w_down, tm):
    a_pad, d = x_sorted.shape
    n_exp, _, de = w_gate.shape
    n_blocks = a_pad // tm
    sub = min(MOE_SUB, tm)

    def blk(b, nu):
        return jnp.minimum(b, nu[0] - 1)

    def frozen(b, f, nu, last):
        return jnp.where(b < nu[0], f, last)

    tf = _pick(de, 256, LANES)
    nf = de // tf
    hid = pl.pallas_call(
        functools.partial(_moe_up_kernel, sub),
        grid_spec=pltpu.PrefetchScalarGridSpec(
            num_scalar_prefetch=3,
            grid=(n_blocks, nf),
            in_specs=[
                pl.BlockSpec((tm, d), lambda b, f, be, nu, nv: (blk(b, nu), 0)),
                pl.BlockSpec((1, d, tf), lambda b, f, be, nu, nv: (be[b], 0, frozen(b, f, nu, nf - 1))),
                pl.BlockSpec((1, d, tf), lambda b, f, be, nu, nv: (be[b], 0, frozen(b, f, nu, nf - 1))),
            ],
            out_specs=pl.BlockSpec((tm, tf), lambda b, f, be, nu, nv: (blk(b, nu), frozen(b, f, nu, nf - 1))),
            scratch_shapes=[pltpu.VMEM((d, 2 * tf), BF16)],
        ),
        out_shape=SDS((a_pad, de), BF16),
        compiler_params=_cparams(("arbitrary", "arbitrary")),
        name="moe_up",
    )(blk_expert, n_used, n_valid_blk, x_sorted, w_gate, w_up)

    tn = _pick(d, 1024, LANES)
    nj = d // tn
    pitch = _row_pitch(d)
    return pl.pallas_call(
        functools.partial(_moe_down_kernel, sub, pitch),
        grid_spec=pltpu.PrefetchScalarGridSpec(
            num_scalar_prefetch=3,
            grid=(n_blocks, nj),
            in_specs=[
                pl.BlockSpec((tm, de), lambda b, j, be, nu, nv: (blk(b, nu), 0)),
                pl.BlockSpec((1, de, tn), lambda b, j, be, nu, nv: (be[b], 0, frozen(b, j, nu, nj - 1))),
            ],
            out_specs=pl.BlockSpec((tm * pitch, LANES), lambda b, j, be, nu, nv: (blk(b, nu), 0)),
            scratch_shapes=[pltpu.VMEM((de, tn), BF16)],
        ),
        out_shape=SDS((a_pad * pitch, LANES), F32),
        compiler_params=_cparams(("arbitrary", "arbitrary")),
        name="moe_down",
    )(blk_expert, n_used, n_valid_blk, hid, w_down)


def _combine_kernel(tile, pitch, n_chunks, pos_ref, h_ref, wt_ref, yp_hbm, h2_ref, h2b_ref, ybuf, sem):
    i = pl.program_id(0)
    n_tiles = pl.num_programs(0)
    n_rows = TOP_K * tile
    slot_rows = n_rows * pitch

    def copy(t, r, slot):
        src = pl.multiple_of(pos_ref[t * n_rows + r] * pitch, SUBLANES)
        dst = pl.multiple_of(slot * slot_rows + r * pitch, SUBLANES)
        return pltpu.make_async_copy(
            yp_hbm.at[pl.ds(src, n_chunks)], ybuf.at[pl.ds(dst, n_chunks)], sem.at[slot])

    def start_tile(t, slot):
        def start(r, c):
            copy(t, r, slot).start()
            return c
        lax.fori_loop(0, n_rows, start, 0)

    @pl.when(i == 0)
    def _():
        start_tile(0, 0)

    slot = i % 2

    @pl.when(i + 1 < n_tiles)
    def _():
        start_tile(i + 1, 1 - slot)

    def wait(r, c):
        copy(i, r, slot).wait()
        return c
    lax.fori_loop(0, n_rows, wait, 0)
    wt = wt_ref[...]
    for c in range(n_chunks):
        cs = slice(c * LANES, (c + 1) * LANES)
        h2 = h_ref[:, cs]
        for kk in range(TOP_K):
            row0 = slot * slot_rows + kk * tile * pitch + c
            h2 = h2 + wt[:, kk:kk + 1] * ybuf[pl.ds(row0, tile, stride=pitch), :]
        h2_ref[:, cs] = h2
        h2b_ref[:, cs] = h2.astype(BF16)


def _combine(h, yp, pos, wts):
    m, d = h.shape
    tile = _pick(m, 256, 16)
    n_tiles = m // tile
    pitch = _row_pitch(d)
    pos_tiled = pos.reshape(n_tiles, tile, TOP_K).transpose(0, 2, 1).reshape(-1)
    return pl.pallas_call(
        functools.partial(_combine_kernel, tile, pitch, d // LANES),
        grid_spec=pltpu.PrefetchScalarGridSpec(
            num_scalar_prefetch=1,
            grid=(n_tiles,),
            in_specs=[
                pl.BlockSpec((tile, d), lambda i, p: (i, 0)),
                pl.BlockSpec((tile, TOP_K), lambda i, p: (i, 0)),
                pl.BlockSpec(memory_space=pl.ANY),
            ],
            out_specs=[pl.BlockSpec((tile, d), lambda i, p: (i, 0)),
                       pl.BlockSpec((tile, d), lambda i, p: (i, 0))],
            scratch_shapes=[pltpu.VMEM((2 * TOP_K * tile * pitch, LANES), F32),
                            pltpu.SemaphoreType.DMA((2,))],
        ),
        out_shape=[SDS((m, d), F32), SDS((m, d), BF16)],
        compiler_params=_cparams(("arbitrary",)),
        name="moe_combine",
    )(pos_tiled, h, wts, yp)


def _ple_kernel(final_norm, nj, hb_ref, h_ref, p_ref, wg_ref, wp_ref, g_ref, o_ref, acc_ref):
    j = pl.program_id(1)
    gate = jax.nn.sigmoid(_dot(hb_ref[...], wg_ref[...]))
    proj = _dot(p_ref[...].astype(BF16), wp_ref[...])
    acc_ref[j] = h_ref[...] + gate * proj

    @pl.when(j == nj - 1)
    def _():
        tn = acc_ref.shape[-1]
        if final_norm:
            ssq = jnp.zeros((acc_ref.shape[1], 1), F32)
            for c in range(nj):
                a = acc_ref[c]
                ssq = ssq + jnp.sum(a * a, axis=-1, keepdims=True)
            rs = lax.rsqrt(ssq / (nj * tn) + EPS)
            for c in range(nj):
                o_ref[:, c * tn:(c + 1) * tn] = acc_ref[c] * rs * g_ref[:, c * tn:(c + 1) * tn]
        else:
            for c in range(nj):
                o_ref[:, c * tn:(c + 1) * tn] = acc_ref[c]


def _ple(h2b, h2, p, w_gate, w_proj, g, rows0, rows, final_norm):
    d = h2.shape[1]
    pd = p.shape[1]
    tm = _pick(math.gcd(rows, rows0) if rows0 else rows, 512, 16)
    tn = _pick(d, 512, LANES)
    nj = d // tn
    ro = rows0 // tm
    return pl.pallas_call(
        functools.partial(_ple_kernel, final_norm, nj),
        grid=(rows // tm, nj),
        in_specs=[
            pl.BlockSpec((tm, d), lambda i, j: (i + ro, 0)),
            pl.BlockSpec((tm, tn), lambda i, j: (i + ro, j)),
            pl.BlockSpec((tm, pd), lambda i, j: (i, 0)),
            pl.BlockSpec((d, tn), lambda i, j: (0, j)),
            pl.BlockSpec((pd, tn), lambda i, j: (0, j)),
            pl.BlockSpec((1, d), lambda i, j: (0, 0)),
        ],
        out_specs=pl.BlockSpec((tm, d), lambda i, j: (i, 0)),
        out_shape=SDS((rows, d), F32),
        scratch_shapes=[pltpu.VMEM((nj, tm, tn), F32)],
        compiler_params=_cparams(("arbitrary", "arbitrary")),
        name="ple_gate",
    )(h2b, h2, p, w_gate, w_proj, g)


def _route(logits, n_groups, epg, tm, tile):
    m = logits.shape[0]
    n_exp = n_groups * epg
    g_logits = logits[:, :n_groups]
    g_idx = jnp.argmax(g_logits, axis=-1)
    g_w = 1.0 / jnp.sum(jnp.exp(g_logits - jnp.max(g_logits, axis=-1, keepdims=True)), axis=-1)
    e_logits = logits[:, n_groups:n_groups + n_exp].reshape(m, n_groups, epg)
    in_group = (jnp.arange(n_groups)[None, :] == g_idx[:, None])[:, :, None]
    e_in = jnp.sum(jnp.where(in_group, e_logits, 0.0), axis=1)
    lane = jnp.arange(epg)[None, :]
    first = jnp.argmax(e_in, axis=-1)
    v1 = jnp.max(e_in, axis=-1)
    rest = jnp.where(lane == first[:, None], NEG_INF, e_in)
    second = jnp.argmax(rest, axis=-1)
    v2 = jnp.max(rest, axis=-1)
    w2 = jnp.exp(v2 - v1)
    e_w = jnp.stack([1.0 / (1.0 + w2), w2 / (1.0 + w2)], axis=-1)
    ids = g_idx[:, None] * epg + jnp.stack([first, second], axis=-1)
    wts = g_w[:, None] * e_w

    a = m * TOP_K
    e_flat = ids.reshape(-1).astype(jnp.int32)
    onehot = (e_flat[:, None] == jnp.arange(n_exp, dtype=jnp.int32)[None, :]).astype(jnp.int32)
    counts = jnp.sum(onehot, axis=0)
    rank = jnp.sum((jnp.cumsum(onehot, axis=0) - onehot) * onehot, axis=1)
    padded = (counts + tm - 1) // tm * tm
    pad_end = jnp.cumsum(padded)
    pad_start = pad_end - padded
    dest = (pad_start[e_flat] + rank).astype(jnp.int32)
    n_blocks = -(-a // tm) + n_exp
    tok = jnp.arange(a, dtype=jnp.int32) // TOP_K
    row_src = jnp.zeros((n_blocks * tm,), jnp.int32).at[dest].set(tok)
    n_used = (pad_end[-1] // tm).astype(jnp.int32)
    blk_id = jnp.arange(n_blocks, dtype=jnp.int32)
    blk_expert = jnp.minimum(jnp.searchsorted(pad_end, blk_id * tm, side="right"), n_exp - 1)
    last = blk_expert[jnp.maximum(n_used - 1, 0)]
    blk_expert = jnp.where(blk_id < n_used, blk_expert, last).astype(jnp.int32)
    tile_id = jnp.arange(n_blocks * tm // tile, dtype=jnp.int32)
    tile_e = blk_expert[tile_id * tile // tm]
    valid_end = (pad_start + counts)[tile_e]
    n_valid = jnp.clip(valid_end - tile_id * tile, 0, tile)
    n_valid = jnp.where(tile_id * tile < pad_end[-1], n_valid, 0).astype(jnp.int32)
    n_used_tiles = (n_used * (tm // tile)).reshape(1)
    n_valid_blk = jnp.sum(n_valid.reshape(n_blocks, tm // tile), axis=1).astype(jnp.int32)
    return (row_src, dest.reshape(m, TOP_K), wts, blk_expert, n_used.reshape(1), n_used_tiles, n_valid,
            n_valid_blk)


def kernel(x_prompt, x_sample, p_prompt, p_sample, cache_ckv, cache_krope, state_C, state_n, state_m, page_table, norm_mix, w_in, b_igate, b_fgate, mlstm_norm, q_a_norm, w_q_b, kv_a_norm, w_kv_b, w_out, norm_ffn, w_group, b_group, w_router, b_router, w_gate, w_up, w_down, w_ple_proj, w_ple_gate, norm_final):
    depth = w_in.shape[0]
    bp, seq, d = x_prompt.shape
    n_dec, ts, _ = x_sample.shape
    mp, ms = bp * seq, n_dec * ts
    m = mp + ms
    hm_heads = b_igate.shape[-1]
    dk, dv = state_n.shape[-1], state_C.shape[-1]
    ql, kvl, rope = q_a_norm.shape[-1], kv_a_norm.shape[-1], cache_krope.shape[-1]
    att_w = d - hm_heads * dv
    a_heads = (w_q_b.shape[-1] - w_kv_b.shape[-1] + att_w) // rope
    av = att_w // a_heads
    n_groups, n_exp = w_group.shape[-1], w_router.shape[-1]
    epg = n_exp // n_groups
    n_pages, page = page_table.shape[1], cache_ckv.shape[2]
    past = n_pages * page
    wq, wv = hm_heads * dk, hm_heads * dv
    assert rope == LANES // 2 and a_heads % 2 == 0 and w_q_b.shape[-1] == a_heads * (A_NOPE + rope)
    assert dv == 2 * dk and mp % ms == 0 and n_groups + n_exp <= LANES and 2 * hm_heads <= LANES // 2

    half = rope // 2
    inv = 1.0 / (ROPE_THETA ** (jnp.arange(half, dtype=F32) / half))
    pos = jnp.concatenate([jnp.tile(jnp.arange(seq), bp), jnp.tile(past + jnp.arange(ts), n_dec)])
    ang = pos.astype(F32)[:, None] * inv[None, :]
    cos_t = jnp.tile(jnp.cos(ang), (1, LANES // half))
    sin_t = jnp.tile(jnp.concatenate([-jnp.sin(ang), jnp.sin(ang)], axis=1), (1, LANES // rope))

    hp = x_prompt.reshape(mp, d)
    hs = x_sample.reshape(ms, d)
    outs = {k: [] for k in ("ckv_p", "kr_p", "C_p", "n_p", "m_p", "ckv_s", "kr_s", "C_s", "n_s", "m_s")}
    for l in range(depth):
        wl = w_in[l]
        o_aq = 2 * wq + 2 * wv + 2 * hm_heads
        o_c = o_aq + ql
        o_kr = o_c + kvl
        w_main = jnp.concatenate([wl[:, :2 * wq + 2 * wv], wl[:, o_aq:o_kr]], axis=1).astype(BF16)
        n_small = rope + 2 * hm_heads
        w_small = jnp.concatenate(
            [wl[:, o_kr:o_kr + rope], wl[:, 2 * wq + 2 * wv:o_aq], jnp.zeros((d, LANES - n_small), F32)],
            axis=1).astype(BF16)
        gate_bias = jnp.concatenate(
            [jnp.zeros((rope,), F32), b_igate[l], b_fgate[l], jnp.zeros((LANES - n_small,), F32)]).reshape(1, LANES)
        wqb = w_q_b[l].reshape(ql, a_heads, A_NOPE + rope)
        wq_re = jnp.concatenate(
            [wqb[:, :, :A_NOPE].reshape(ql, -1), wqb[:, :, A_NOPE:].reshape(ql, -1)], axis=1).astype(BF16)
        wkvb = w_kv_b[l].astype(BF16)
        wkv3 = w_kv_b[l].reshape(kvl, a_heads, A_NOPE + av)
        w_ukt = wkv3[:, :, :A_NOPE].transpose(1, 2, 0).astype(BF16)
        w_uv = wkv3[:, :, A_NOPE:].transpose(1, 0, 2).astype(BF16)
        w_o1 = w_out[l][:wv].astype(BF16)
        w_o2 = w_out[l][wv:].astype(BF16)
        w_rg = jnp.concatenate(
            [w_group[l], w_router[l], jnp.zeros((d, LANES - n_groups - n_exp), F32)], axis=1)
        b_rg = jnp.concatenate(
            [b_group[l], b_router[l], jnp.zeros((LANES - n_groups - n_exp,), F32)]).reshape(1, LANES)
        w_pg = w_ple_gate[l].astype(BF16)
        w_pp = w_ple_proj[l].astype(BF16)
        g_mix = norm_mix[l].reshape(1, d)

        zz = _in_proj(hp, g_mix, w_main, w_small, m, 0, None)
        z, zs = _in_proj(hs, g_mix, w_main, w_small, m, mp, zz)
        q_hm, ckv, kr = _mla_q(z, zs, cos_t, sin_t, q_a_norm[l].reshape(1, ql), kv_a_norm[l].reshape(1, kvl),
                               wq_re, a_heads, ql, kvl, rope, 2 * wq + 2 * wv, 2 * wq + 2 * wv + ql)
        ckv_p, ckv_s, kr_p, kr_s = ckv[:mp], ckv[mp:], kr[:mp], kr[mp:]

        norm_m = mlstm_norm[l].reshape(1, wv)
        lp = _pick(seq, 256, 16)
        hm, c_p, n_p, m_p = _mlstm(z, zs, gate_bias, norm_m, bp, seq, lp, 1, 0, hm_heads, dk, dv, m, None, None)
        n_sub = 2 if n_dec % 2 == 0 else 1
        hm, c_s, n_s, m_s = _mlstm(z, zs, gate_bias, norm_m, n_dec, ts, ts, n_sub, mp, hm_heads, dk, dv, m,
                                   (state_C[l], state_n[l], state_m[l].reshape(n_dec, 1, hm_heads)), hm)

        k_hm, v_hm = _kv_expand(ckv_p, kr_p, wkvb, mp, a_heads, av)
        att = _flash_prompt(q_hm, k_hm, v_hm, bp, seq, m)
        qs = _q_latent(q_hm, w_ukt, mp, n_dec, ts)
        o_s = _mla_sample(page_table, qs, ckv_s, kr_s, cache_ckv[l], jnp.swapaxes(cache_krope[l], 1, 2),
                          (A_NOPE + rope) ** -0.5)
        att = _o_uv(o_s, w_uv, att, mp)

        h1 = _out_proj(hm, att, w_o1, w_o2, hp, 0, m, None)
        h1 = _out_proj(hm, att, w_o1, w_o2, hs, mp, m, h1)

        xn2p, logits = _router(h1, norm_ffn[l].reshape(1, d), w_rg, b_rg)
        row_src, posn, wts, blk_expert, n_used, n_used_tiles, n_valid, n_valid_blk = _route(
            logits, n_groups, epg, MOE_BLOCK, MOE_TILE)
        x_sorted = _dispatch(xn2p, row_src, n_used_tiles, n_valid, MOE_TILE, d)
        yp = _moe_ffn(x_sorted, blk_expert, n_used, n_valid_blk, w_gate[l], w_up[l], w_down[l], MOE_BLOCK)
        h2, h2b = _combine(h1, yp, posn, wts)

        last = l == depth - 1
        g_fin = norm_final.reshape(1, d)
        hp = _ple(h2b, h2, p_prompt[l].reshape(mp, -1), w_pg, w_pp, g_fin, 0, mp, last)
        hs = _ple(h2b, h2, p_sample[l].reshape(ms, -1), w_pg, w_pp, g_fin, mp, ms, last)

        outs["ckv_p"].append(ckv_p.reshape(bp, seq, kvl))
        outs["kr_p"].append(kr_p.reshape(bp, seq, rope))
        outs["C_p"].append(c_p)
        outs["n_p"].append(n_p)
        outs["m_p"].append(m_p.reshape(bp, hm_heads))
        outs["ckv_s"].append(ckv_s.reshape(n_dec, ts, kvl))
        outs["kr_s"].append(kr_s.reshape(n_dec, ts, rope))
        outs["C_s"].append(c_s)
        outs["n_s"].append(n_s)
        outs["m_s"].append(m_s.reshape(n_dec, hm_heads))

    st = {k: jnp.stack(v) for k, v in outs.items()}
    return (hp.reshape(bp, seq, d), hs.reshape(n_dec, ts, d),
            st["ckv_p"], st["kr_p"], st["C_p"], st["n_p"], st["m_p"],
            st["ckv_s"], st["kr_s"], st["C_s"], st["n_s"], st["m_s"])
```

```python
import functools
import math

import jax
import jax.numpy as jnp
from jax import lax
from jax.experimental import pallas as pl
from jax.experimental.pallas import tpu as pltpu

F32 = jnp.float32
BF16 = jnp.bfloat16
SDS = jax.ShapeDtypeStruct

EPS = 1e-6
GATE_CAP = 15.0
ROPE_THETA = 10000.0
A_NOPE = 128
TOP_K = 2

LANES = 128
SUBLANES = 8
VMEM_LIMIT = 56 * 1024 * 1024
NEG_INF = float("-inf")

NT_DIMS = (((1,), (1,)), ((), ()))


def _cparams(sem, vmem=VMEM_LIMIT):
    return pltpu.CompilerParams(dimension_semantics=sem, vmem_limit_bytes=vmem)


def _pick(n, pref, mult):
    t = min(pref, n) // mult * mult
    while t >= mult:
        if n % t == 0:
            return t
        t -= mult
    return n


def _rup(x, m):
    return (x + m - 1) // m * m


def _dot(a, b):
    return jnp.dot(a, b, preferred_element_type=F32)


def _dot_nt(a, b, precision=None):
    return lax.dot_general(a, b, NT_DIMS, preferred_element_type=F32, precision=precision)


def _rms(x, g):
    ms = jnp.mean(x * x, axis=-1, keepdims=True)
    return x * lax.rsqrt(ms + EPS) * g


def _inproj_kernel(x_ref, g_ref, w_ref, ws_ref, *rest):
    z_ref, zs_ref, xn_ref = rest[-3:]
    j = pl.program_id(1)

    @pl.when(j == 0)
    def _():
        xn = _rms(x_ref[...], g_ref[...]).astype(BF16)
        xn_ref[...] = xn
        zs_ref[...] = _dot(xn, ws_ref[...])

    z_ref[...] = _dot(xn_ref[...], w_ref[...])


def _in_proj(x, g, w_main, w_small, m_total, row0, prev):
    rows, d = x.shape
    nm = w_main.shape[1]
    tm = _pick(math.gcd(rows, row0) if row0 else rows, 512, 16)
    tn = _pick(nm, 512, LANES)
    ro = row0 // tm
    ins = [x, g, w_main, w_small]
    in_specs = [
        pl.BlockSpec((tm, d), lambda i, j: (i, 0)),
        pl.BlockSpec((1, d), lambda i, j: (0, 0)),
        pl.BlockSpec((d, tn), lambda i, j: (0, j)),
        pl.BlockSpec((d, LANES), lambda i, j: (0, 0)),
    ]
    aliases = {}
    if prev is not None:
        ins += list(prev)
        in_specs += [pl.BlockSpec(memory_space=pl.ANY)] * 2
        aliases = {4: 0, 5: 1}
    return pl.pallas_call(
        _inproj_kernel,
        grid=(rows // tm, nm // tn),
        in_specs=in_specs,
        out_specs=[pl.BlockSpec((tm, tn), lambda i, j: (i + ro, j)),
                   pl.BlockSpec((tm, LANES), lambda i, j: (i + ro, 0))],
        out_shape=[SDS((m_total, nm), F32), SDS((m_total, LANES), F32)],
        scratch_shapes=[pltpu.VMEM((tm, d), BF16)],
        input_output_aliases=aliases,
        compiler_params=_cparams(("arbitrary", "arbitrary")),
        name="in_proj",
    )(*ins)


def _rope_lanes(y, cos, sin_signed, width):
    half = 32
    lane = lax.broadcasted_iota(jnp.int32, y.shape, 1)
    first = (lane % (2 * half)) < half
    swapped = jnp.where(first, pltpu.roll(y, width - half, 1), pltpu.roll(y, half, 1))
    return y * cos + swapped * sin_signed


def _mlaq_kernel(n_heads, rope, aq_ref, c_ref, zs_ref, cos_ref, sin_ref, qg_ref, kvg_ref, wq_ref,
                 q_ref, ckv_ref, kr_ref):
    aqn = _rms(aq_ref[...], qg_ref[...]).astype(BF16)
    y = _dot(aqn, wq_ref[...])
    cos = cos_ref[...]
    sin = sin_ref[...]
    nope_w = n_heads * A_NOPE
    for h2 in range(n_heads // 2):
        yr = y[:, nope_w + h2 * LANES: nope_w + (h2 + 1) * LANES]
        rot = _rope_lanes(yr, cos, sin, LANES)
        for s in range(2):
            h = 2 * h2 + s
            q_ref[h, :, 0:A_NOPE] = y[:, h * A_NOPE:(h + 1) * A_NOPE].astype(BF16)
            q_ref[h, :, A_NOPE:A_NOPE + rope] = rot[:, s * rope:(s + 1) * rope].astype(BF16)
    ckv_ref[...] = _rms(c_ref[...], kvg_ref[...])
    kr = _rope_lanes(zs_ref[...], cos, sin, LANES)
    kr_ref[...] = kr[:, 0:rope]


def _mla_q(z, zs, cos, sin, qg, kvg, wq, n_heads, ql, kvl, rope, aq_off, c_off):
    m = z.shape[0]
    tm = _pick(m, 256, 16)
    qd = A_NOPE + rope
    return pl.pallas_call(
        functools.partial(_mlaq_kernel, n_heads, rope),
        grid=(m // tm,),
        in_specs=[
            pl.BlockSpec((tm, ql), lambda i: (i, aq_off // ql)),
            pl.BlockSpec((tm, kvl), lambda i: (i, c_off // kvl)),
            pl.BlockSpec((tm, LANES), lambda i: (i, 0)),
            pl.BlockSpec((tm, LANES), lambda i: (i, 0)),
            pl.BlockSpec((tm, LANES), lambda i: (i, 0)),
            pl.BlockSpec((1, ql), lambda i: (0, 0)),
            pl.BlockSpec((1, kvl), lambda i: (0, 0)),
            pl.BlockSpec((ql, n_heads * qd), lambda i: (0, 0)),
        ],
        out_specs=[
            pl.BlockSpec((n_heads, tm, qd), lambda i: (0, i, 0)),
            pl.BlockSpec((tm, kvl), lambda i: (i, 0)),
            pl.BlockSpec((tm, rope), lambda i: (i, 0)),
        ],
        out_shape=[SDS((n_heads, m, qd), BF16), SDS((m, kvl), F32), SDS((m, rope), F32)],
        compiler_params=_cparams(("arbitrary",)),
        name="mla_q",
    )(z, z, zs, cos, sin, qg, kvg, wq)


def _kv_kernel(n_heads, av, rope, c_ref, kr_ref, w_ref, k_ref, v_ref):
    y = _dot(c_ref[...].astype(BF16), w_ref[...])
    kr = kr_ref[...].astype(BF16)
    hw = A_NOPE + av
    for h in range(n_heads):
        k_ref[h, :, 0:A_NOPE] = y[:, h * hw: h * hw + A_NOPE].astype(BF16)
        k_ref[h, :, A_NOPE:A_NOPE + rope] = kr
        v_ref[h] = y[:, h * hw + A_NOPE:(h + 1) * hw].astype(BF16)


def _kv_expand(ckv, kr, w_kvb, rows, n_heads, av):
    kvl = ckv.shape[1]
    rope = kr.shape[1]
    tm = _pick(rows, 256, 16)
    return pl.pallas_call(
        functools.partial(_kv_kernel, n_heads, av, rope),
        grid=(rows // tm,),
        in_specs=[
            pl.BlockSpec((tm, kvl), lambda i: (i, 0)),
            pl.BlockSpec((tm, rope), lambda i: (i, 0)),
            pl.BlockSpec(w_kvb.shape, lambda i: (0, 0)),
        ],
        out_specs=[
            pl.BlockSpec((n_heads, tm, A_NOPE + rope), lambda i: (0, i, 0)),
            pl.BlockSpec((n_heads, tm, av), lambda i: (0, i, 0)),
        ],
        out_shape=[SDS((n_heads, rows, A_NOPE + rope), BF16), SDS((n_heads, rows, av), BF16)],
        compiler_params=_cparams(("arbitrary",)),
        name="kv_expand",
    )(ckv, kr, w_kvb)


def _flash_kernel(tq, tk, scale, q_ref, k_ref, v_ref, o_ref):
    qi = pl.program_id(2)
    q = q_ref[0]
    av = v_ref.shape[-1]
    row = qi * tq + lax.broadcasted_iota(jnp.int32, (tq, tk), 0)
    col0 = lax.broadcasted_iota(jnp.int32, (tq, tk), 1)

    def body(masked, kb, carry):
        m, l, acc = carry
        start = pl.multiple_of(kb * tk, tk)
        k = k_ref[0, pl.ds(start, tk), :]
        v = v_ref[0, pl.ds(start, tk), :]
        s = _dot_nt(q, k) * scale
        if masked:
            s = jnp.where(col0 + kb * tk <= row, s, NEG_INF)
        m_new = jnp.maximum(m, jnp.max(s, axis=-1, keepdims=True))
        p = jnp.exp(s - m_new)
        alpha = jnp.exp(m - m_new)
        l = alpha * l + jnp.sum(p, axis=-1, keepdims=True)
        acc = alpha * acc + _dot(p.astype(BF16), v)
        return m_new, l, acc

    n_full = (qi * tq) // tk
    n_kb = (qi * tq + tq + tk - 1) // tk
    init = (jnp.full((tq, 1), NEG_INF, F32), jnp.zeros((tq, 1), F32), jnp.zeros((tq, av), F32))
    carry = lax.fori_loop(0, n_full, functools.partial(body, False), init)
    _, l, acc = lax.fori_loop(n_full, n_kb, functools.partial(body, True), carry)
    o_ref[...] = (acc / l).astype(o_ref.dtype)


def _flash_prompt(q_hm, k_hm, v_hm, n_batch, seq, m_total):
    n_heads, _, qd = q_hm.shape
    av = v_hm.shape[-1]
    tq = _pick(seq, 512, 16)
    tk = _pick(seq, 512, 16)
    nq = seq // tq
    return pl.pallas_call(
        functools.partial(_flash_kernel, tq, tk, qd ** -0.5),
        grid=(n_batch, n_heads, nq),
        in_specs=[
            pl.BlockSpec((1, tq, qd), lambda b, h, i: (h, b * nq + i, 0)),
            pl.BlockSpec((1, seq, qd), lambda b, h, i: (h, b, 0)),
            pl.BlockSpec((1, seq, av), lambda b, h, i: (h, b, 0)),
        ],
        out_specs=pl.BlockSpec((tq, av), lambda b, h, i: (b * nq + i, h)),
        out_shape=SDS((m_total, n_heads * av), BF16),
        compiler_params=_cparams(("arbitrary", "arbitrary", "arbitrary")),
        name="flash_prompt",
    )(q_hm, k_hm, v_hm)


def _qlat_kernel(n_seq, ts, kvl, rope, q_ref, w_ref, o_ref):
    q = q_ref[0]
    qlat = _dot(q[:, 0:A_NOPE], w_ref[0])
    width = o_ref.shape[-1]
    full = jnp.concatenate(
        [qlat, q[:, A_NOPE:A_NOPE + rope].astype(F32),
         jnp.zeros((q.shape[0], width - kvl - rope), F32)], axis=1)
    o_ref[...] = full.reshape(n_seq, 1, ts, width)


def _q_latent(q_hm, w_ukt, rows0, n_seq, ts):
    n_heads, _, qd = q_hm.shape
    kvl = w_ukt.shape[-1]
    rope = qd - A_NOPE
    ms = n_seq * ts
    width = _rup(kvl + rope, LANES)
    return pl.pallas_call(
        functools.partial(_qlat_kernel, n_seq, ts, kvl, rope),
        grid=(n_heads,),
        in_specs=[
            pl.BlockSpec((1, ms, qd), lambda h: (h, rows0 // ms, 0)),
            pl.BlockSpec((1, A_NOPE, kvl), lambda h: (h, 0, 0)),
        ],
        out_specs=pl.BlockSpec((n_seq, 1, ts, width), lambda h: (0, h, 0, 0)),
        out_shape=SDS((n_seq, n_heads, ts, width), F32),
        compiler_params=_cparams(("arbitrary",)),
        name="q_latent",
    )(q_hm, w_ukt)


MLA_BUFFERS = 3
MLA_STREAMS = 4


def _mla_sample_kernel(n_pages, chunk_pages, page, n_seq, ts, kvl, rope, scale,
                       pt_ref, q_ref, cn_ref, kn_ref, ckv_hbm, krt_hbm, o_ref,
                       kbuf, rbuf, sem, *stats):
    m_ref, l_ref, acc_ref = stats[0::3], stats[1::3], stats[2::3]
    b = pl.program_id(0)
    n_chunks = n_pages // chunk_pages
    n_total = n_seq * n_chunks
    rows = q_ref.shape[1] * ts
    keys = chunk_pages * page
    part = keys // MLA_STREAMS

    def chunk_copies(g, slot):
        gw = g % n_total
        base = (gw // n_chunks) * n_pages + (gw % n_chunks) * chunk_pages
        copies = []
        for p in range(chunk_pages):
            pg = pt_ref[base + p]
            copies.append(pltpu.make_async_copy(
                ckv_hbm.at[pg], kbuf.at[slot, pl.ds(p * page, page)], sem.at[slot]))
            copies.append(pltpu.make_async_copy(
                krt_hbm.at[pg], rbuf.at[slot, :, pl.ds(p * page, page)], sem.at[slot]))
        return copies

    def start_chunk(g, slot):
        for n, cp in enumerate(chunk_copies(g, slot)):
            cp.start(priority=(n // 2 + n) % 2)

    @pl.when(b == 0)
    def _():
        for g0 in range(MLA_BUFFERS - 1):
            start_chunk(g0, g0)

    q = q_ref[0].reshape(rows, q_ref.shape[-1])
    ql = q[:, 0:kvl].astype(BF16)
    qr = q[:, kvl:kvl + rope].astype(BF16)
    for j in range(MLA_STREAMS):
        m_ref[j][...] = jnp.full(m_ref[j].shape, NEG_INF, F32)
        l_ref[j][...] = jnp.zeros(l_ref[j].shape, F32)
        acc_ref[j][...] = jnp.zeros(acc_ref[j].shape, F32)

    def softmax_step(j, s):
        m = m_ref[j][...]
        m_new = jnp.maximum(m, jnp.max(s, axis=-1, keepdims=True))
        p = jnp.exp(s - m_new)
        alpha = jnp.exp(m - m_new)
        l_ref[j][...] = alpha * l_ref[j][...] + jnp.sum(p, axis=-1, keepdims=True)
        m_ref[j][...] = m_new
        return p.astype(BF16), alpha

    def accumulate(j, p, alpha, values):
        acc_ref[j][...] = alpha * acc_ref[j][...] + _dot(p, values)

    def body(c, carry):
        g = b * n_chunks + c
        slot = g % MLA_BUFFERS
        ahead = g + MLA_BUFFERS - 1
        start_chunk(ahead, ahead % MLA_BUFFERS)
        for cp in chunk_copies(g, slot):
            cp.wait()
        ks = [kbuf[slot, j * part:(j + 1) * part, :].astype(BF16) for j in range(MLA_STREAMS)]
        ss = [(_dot_nt(ql, ks[j]) + _dot(qr, rbuf[slot, :, j * part:(j + 1) * part].astype(BF16))) * scale
              for j in range(MLA_STREAMS)]
        pa = [softmax_step(j, ss[j]) for j in range(MLA_STREAMS)]
        for j in range(MLA_STREAMS):
            accumulate(j, pa[j][0], pa[j][1], ks[j])
        return carry

    lax.fori_loop(0, n_chunks, body, 0)

    @pl.when(b == n_seq - 1)
    def _():
        for d in range(MLA_BUFFERS - 1):
            for cp in chunk_copies(n_total + d, (n_total + d) % MLA_BUFFERS):
                cp.wait()

    pad = 2 * SUBLANES - ts
    cn = jnp.concatenate([cn_ref[...], jnp.zeros((pad, kvl), F32)], axis=0).astype(BF16)
    kn = jnp.concatenate([kn_ref[...], jnp.zeros((pad, rope), F32)], axis=0).astype(BF16)
    s = (_dot_nt(ql, cn) + _dot_nt(qr, kn)) * scale
    tok = lax.broadcasted_iota(jnp.int32, s.shape, 0) % ts
    col = lax.broadcasted_iota(jnp.int32, s.shape, 1)
    s = jnp.where(col <= tok, s, NEG_INF)
    p_new, alpha_new = softmax_step(0, s)
    accumulate(0, p_new, alpha_new, cn)

    m_all = m_ref[0][...]
    for j in range(1, MLA_STREAMS):
        m_all = jnp.maximum(m_all, m_ref[j][...])
    l_all = jnp.zeros(m_all.shape, F32)
    acc = jnp.zeros(acc_ref[0].shape, F32)
    for j in range(MLA_STREAMS):
        w = jnp.exp(m_ref[j][...] - m_all)
        l_all = l_all + w * l_ref[j][...]
        acc = acc + w * acc_ref[j][...]
    o_ref[0] = (acc / l_all).reshape(o_ref.shape[1:])


def _mla_sample(page_table, qs, ckv_s, kr_s, cache_ckv, cache_krt, scale):
    n_seq, n_heads, ts, width = qs.shape
    n_pages = page_table.shape[1]
    page, kvl = cache_ckv.shape[1:]
    rope = cache_krt.shape[1]
    chunk_pages = _pick(n_pages, 8, 1)
    rows = n_heads * ts
    keys = chunk_pages * page
    assert n_seq * (n_pages // chunk_pages) >= MLA_BUFFERS and keys % (MLA_STREAMS * LANES) == 0
    kern = functools.partial(_mla_sample_kernel, n_pages, chunk_pages, page, n_seq, ts, kvl, rope, scale)
    return pl.pallas_call(
        kern,
        grid_spec=pltpu.PrefetchScalarGridSpec(
            num_scalar_prefetch=1,
            grid=(n_seq,),
            in_specs=[
                pl.BlockSpec((1, n_heads, ts, width), lambda b, pt: (b, 0, 0, 0)),
                pl.BlockSpec((ts, kvl), lambda b, pt: (b, 0)),
                pl.BlockSpec((ts, rope), lambda b, pt: (b, 0)),
                pl.BlockSpec(memory_space=pl.ANY),
                pl.BlockSpec(memory_space=pl.ANY),
            ],
            out_specs=pl.BlockSpec((1, n_heads, ts, kvl), lambda b, pt: (b, 0, 0, 0)),
            scratch_shapes=[
                pltpu.VMEM((MLA_BUFFERS, keys, kvl), F32),
                pltpu.VMEM((MLA_BUFFERS, rope, keys), F32),
                pltpu.SemaphoreType.DMA((MLA_BUFFERS,)),
            ] + MLA_STREAMS * [pltpu.VMEM((rows, 1), F32), pltpu.VMEM((rows, 1), F32),
                               pltpu.VMEM((rows, kvl), F32)],
        ),
        out_shape=SDS((n_seq, n_heads, ts, kvl), F32),
        compiler_params=_cparams(("arbitrary",)),
        name="mla_sample",
    )(page_table.reshape(-1), qs, ckv_s, kr_s, cache_ckv, cache_krt)


def _ouv_kernel(o_ref, w_ref, prev_ref, att_ref):
    del prev_ref
    o = o_ref[...]
    o = o.reshape(o.shape[0] * o.shape[2], o.shape[3]).astype(BF16)
    att_ref[...] = _dot(o, w_ref[0]).astype(att_ref.dtype)


def _o_uv(o_s, w_uv, att, rows0):
    n_seq, n_heads, ts, kvl = o_s.shape
    av = w_uv.shape[-1]
    ms = n_seq * ts
    return pl.pallas_call(
        _ouv_kernel,
        grid=(n_heads,),
        in_specs=[
            pl.BlockSpec((n_seq, 1, ts, kvl), lambda h: (0, h, 0, 0)),
            pl.BlockSpec((1, kvl, av), lambda h: (h, 0, 0)),
            pl.BlockSpec(memory_space=pl.ANY),
        ],
        out_specs=pl.BlockSpec((ms, av), lambda h: (rows0 // ms, h)),
        out_shape=SDS(att.shape, att.dtype),
        input_output_aliases={2: 0},
        compiler_params=_cparams(("arbitrary",)),
        name="o_uv",
    )(o_s, w_uv, att)


def _mlstm_kernel(n_sub, L, n_heads, dk, dv, has_state, *refs):
    if has_state:
        (q_ref, k_ref, v_ref, mo_ref, zs_ref, bias_ref, norm_ref, c0_ref, n0_ref, m0_ref, _prev,
         hm_ref, c_ref, n_ref, m_ref) = refs
    else:
        (q_ref, k_ref, v_ref, mo_ref, zs_ref, bias_ref, norm_ref,
         hm_ref, c_ref, n_ref, m_ref) = refs
    ci = pl.program_id(1)
    Lk = _rup(L, LANES)

    @pl.when(ci == 0)
    def _():
        if has_state:
            c_ref[...] = c0_ref[...]
            n_ref[...] = n0_ref[...]
            m_ref[...] = m0_ref[...]
        else:
            c_ref[...] = jnp.zeros(c_ref.shape, F32)
            n_ref[...] = jnp.zeros(n_ref.shape, F32)
            m_ref[...] = jnp.zeros(m_ref.shape, F32)

    def pad_rows(a):
        if Lk == L:
            return a
        return jnp.concatenate([a, jnp.zeros((Lk - L, a.shape[1]), a.dtype)], axis=0)

    row = lax.broadcasted_iota(jnp.int32, (L, Lk), 0)
    col = lax.broadcasted_iota(jnp.int32, (L, Lk), 1)
    causal = col <= row
    tril = causal.astype(F32)
    sel_r = lax.broadcasted_iota(jnp.int32, (SUBLANES, LANES), 0)
    sel_c = lax.broadcasted_iota(jnp.int32, (SUBLANES, LANES), 1)
    lane0 = LANES // 2
    sel = ((sel_c == sel_r + lane0) & (sel_r < n_heads)).astype(F32)
    eye = (lax.broadcasted_iota(jnp.int32, (dk, dk), 0)
           == lax.broadcasted_iota(jnp.int32, (dk, dk), 1)).astype(BF16)
    hi = lax.Precision.HIGHEST

    for sq in range(n_sub):
        rs = slice(sq * L, (sq + 1) * L)
        pre = zs_ref[rs, :] + bias_ref[...]
        cap = GATE_CAP * jnp.tanh(pre / GATE_CAP)
        i_all = cap
        f_all = jnp.minimum(cap, 0.0) - jnp.log1p(jnp.exp(-jnp.abs(cap)))
        bcum_all = jnp.dot(tril, pad_rows(f_all), preferred_element_type=F32, precision=hi)
        r_all = i_all - pltpu.roll(bcum_all, LANES - n_heads, 1)
        r_rows = _dot_nt(sel, pad_rows(r_all), precision=hi)
        for h in range(n_heads):
            i_col = i_all[:, lane0 + h: lane0 + h + 1]
            b_col = bcum_all[:, lane0 + n_heads + h: lane0 + n_heads + h + 1]
            r_row = r_rows[h:h + 1, :]
            b_tot = b_col[L - 1:L, :]
            m_prev = m_ref[sq, :, h:h + 1]
            c_prev = c_ref[sq, h]
            n_prev = n_ref[sq, h:h + 1, :]

            q = q_ref[rs, h * dk:(h + 1) * dk] * (dk ** -0.5)
            k = k_ref[rs, h * dk:(h + 1) * dk]
            v = v_ref[rs, h * dv:(h + 1) * dv]
            qb = q.astype(BF16)
            kpb = pad_rows(k).astype(BF16)
            vpb = pad_rows(v).astype(BF16)

            dlog = jnp.where(causal, b_col + r_row, NEG_INF)
            inter = b_col + m_prev
            m_t = jnp.maximum(inter, jnp.max(dlog, axis=-1, keepdims=True))
            s = _dot_nt(qb, kpb) * jnp.exp(dlog - m_t)
            w_prev = jnp.exp(inter - m_t)
            num = _dot(s.astype(BF16), vpb) + w_prev * _dot(qb, c_prev.astype(BF16))
            den = (jnp.sum(s, axis=-1, keepdims=True)
                   + w_prev * jnp.sum(q * n_prev, axis=-1, keepdims=True))
            hout = num / jnp.maximum(jnp.abs(den), jnp.exp(-m_t))

            g = b_tot - b_col + i_col
            m_new = jnp.maximum(b_tot + m_prev, jnp.max(g, axis=0, keepdims=True))
            wk = jnp.exp(g - m_new)
            decay = jnp.exp(b_tot + m_prev - m_new)
            kw = wk * k
            kw_t = _dot_nt(eye, pad_rows(kw).astype(BF16)).astype(BF16)
            c_ref[sq, h] = decay * c_prev + _dot(kw_t, vpb)
            n_ref[sq, h:h + 1, :] = decay * n_prev + jnp.sum(kw, axis=0, keepdims=True)
            m_ref[sq, :, h:h + 1] = m_new

            hn = _rms(hout, norm_ref[:, h * dv:(h + 1) * dv])
            hn = hn * jax.nn.sigmoid(mo_ref[rs, h * dv:(h + 1) * dv])
            hm_ref[rs, h * dv:(h + 1) * dv] = hn.astype(hm_ref.dtype)


def _mlstm(z, zs, bias, norm, n_seq, seq, L, n_sub, rows0, n_heads, dk, dv, m_total, state, prev_hm):
    wq = n_heads * dk
    wv = n_heads * dv
    n_chunks = seq // L
    rb = n_sub * L
    has_state = state is not None
    if has_state:
        assert n_chunks == 1
    ro = rows0 // rb

    def rmap(cb):
        return lambda b, c: (ro + b * n_chunks + c, cb)

    ins = [z, z, z, z, zs, bias, norm]
    in_specs = [
        pl.BlockSpec((rb, wq), rmap(0)),
        pl.BlockSpec((rb, wq), rmap(1)),
        pl.BlockSpec((rb, wv), rmap(1)),
        pl.BlockSpec((rb, wv), rmap(2)),
        pl.BlockSpec((rb, LANES), rmap(0)),
        pl.BlockSpec((1, LANES), lambda b, c: (0, 0)),
        pl.BlockSpec((1, wv), lambda b, c: (0, 0)),
    ]
    c_spec = pl.BlockSpec((n_sub, n_heads, dk, dv), lambda b, c: (b, 0, 0, 0))
    n_spec = pl.BlockSpec((n_sub, n_heads, dk), lambda b, c: (b, 0, 0))
    m_spec = pl.BlockSpec((n_sub, 1, n_heads), lambda b, c: (b, 0, 0))
    aliases = {}
    if has_state:
        ins += [state[0], state[1], state[2], prev_hm]
        in_specs += [c_spec, n_spec, m_spec, pl.BlockSpec(memory_space=pl.ANY)]
        aliases = {10: 0}
    return pl.pallas_call(
        functools.partial(_mlstm_kernel, n_sub, L, n_heads, dk, dv, has_state),
        grid=(n_seq // n_sub, n_chunks),
        in_specs=in_specs,
        out_specs=[pl.BlockSpec((rb, wv), rmap(0)), c_spec, n_spec, m_spec],
        out_shape=[SDS((m_total, wv), BF16), SDS((n_seq, n_heads, dk, dv), F32),
                   SDS((n_seq, n_heads, dk), F32), SDS((n_seq, 1, n_heads), F32)],
        input_output_aliases=aliases,
        compiler_params=_cparams(("arbitrary", "arbitrary")),
        name="mlstm_state" if has_state else "mlstm_prompt",
    )(*ins)


def _outproj_kernel(hm_ref, att_ref, w1_ref, w2_ref, x_ref, *rest):
    o_ref = rest[-1]
    o_ref[...] = x_ref[...] + _dot(hm_ref[...], w1_ref[...]) + _dot(att_ref[...], w2_ref[...])


def _out_proj(hm, att, w, x, rows0, m_total, prev):
    rows, d = x.shape
    k1, k2 = hm.shape[1], att.shape[1]
    assert k1 == k2 and w.shape[0] == k1 + k2
    tm = _pick(math.gcd(rows, rows0) if rows0 else rows, 1024, 16)
    tn = _pick(d, 512, LANES)
    ro = rows0 // tm
    ins = [hm, att, w, w, x]
    in_specs = [
        pl.BlockSpec((tm, k1), lambda i, j: (i + ro, 0)),
        pl.BlockSpec((tm, k2), lambda i, j: (i + ro, 0)),
        pl.BlockSpec((k1, tn), lambda i, j: (0, j)),
        pl.BlockSpec((k2, tn), lambda i, j: (1, j)),
        pl.BlockSpec((tm, tn), lambda i, j: (i, j)),
    ]
    aliases = {}
    if prev is not None:
        ins.append(prev)
        in_specs.append(pl.BlockSpec(memory_space=pl.ANY))
        aliases = {5: 0}
    return pl.pallas_call(
        _outproj_kernel,
        grid=(rows // tm, d // tn),
        in_specs=in_specs,
        out_specs=pl.BlockSpec((tm, tn), lambda i, j: (i + ro, j)),
        out_shape=SDS((m_total, d), F32),
        input_output_aliases=aliases,
        compiler_params=_cparams(("arbitrary", "arbitrary")),
        name="out_proj",
    )(*ins)


MOE_BLOCK = 512
MOE_SUB = 256
MOE_TILE = 256


def _row_pitch(d):
    return d // LANES + SUBLANES


def _store_token_rows(ref2d, val, row0, pitch):
    tm = val.shape[0]
    for cc in range(val.shape[1] // LANES):
        ref2d[pl.ds(row0 + cc, tm, stride=pitch), :] = val[:, cc * LANES:(cc + 1) * LANES]


def _router_kernel(pitch, h_ref, g_ref, w_ref, b_ref, xp_ref, lg_ref):
    xn = _rms(h_ref[...], g_ref[...])
    _store_token_rows(xp_ref, xn, 0, pitch)
    lg_ref[...] = jnp.dot(xn, w_ref[...], preferred_element_type=F32,
                          precision=lax.Precision.HIGHEST) + b_ref[...]


def _router(h, g, w_rg, b_rg):
    m, d = h.shape
    tm = _pick(m, 256, 16)
    pitch = _row_pitch(d)
    return pl.pallas_call(
        functools.partial(_router_kernel, pitch),
        grid=(m // tm,),
        in_specs=[
            pl.BlockSpec((tm, d), lambda i: (i, 0)),
            pl.BlockSpec((1, d), lambda i: (0, 0)),
            pl.BlockSpec((d, LANES), lambda i: (0, 0)),
            pl.BlockSpec((1, LANES), lambda i: (0, 0)),
        ],
        out_specs=[pl.BlockSpec((tm * pitch, LANES), lambda i: (i, 0)),
                   pl.BlockSpec((tm, LANES), lambda i: (i, 0))],
        out_shape=[SDS((m * pitch, LANES), F32), SDS((m, LANES), F32)],
        compiler_params=_cparams(("arbitrary",)),
        name="ffn_router",
    )(h, g, w_rg, b_rg)


def _dispatch_kernel(tile, pitch, n_chunks, nu_ref, nv_ref, idx_ref, xp_hbm, o_ref, stage, sem):
    i = pl.program_id(0)
    nu = nu_ref[0]
    slot_rows = tile * pitch

    def copy(t, r, slot):
        src = pl.multiple_of(idx_ref[t * tile + r] * pitch, SUBLANES)
        dst = pl.multiple_of(slot * slot_rows + r * pitch, SUBLANES)
        return pltpu.make_async_copy(
            xp_hbm.at[pl.ds(src, n_chunks)], stage.at[pl.ds(dst, n_chunks)], sem.at[slot])

    def start_tile(t, slot):
        nv = nv_ref[t]

        def start(r2, c):
            copy(t, 2 * r2, slot).start(priority=0)
            copy(t, 2 * r2 + 1, slot).start(priority=1)
            return c
        lax.fori_loop(0, nv // 2, start, 0)

        @pl.when(nv % 2 == 1)
        def _():
            copy(t, nv - 1, slot).start(priority=0)

    @pl.when(i == 0)
    def _():
        stage[...] = jnp.zeros(stage.shape, F32)
        start_tile(0, 0)

    @pl.when(i < nu)
    def _():
        slot = i % 2

        @pl.when(i + 1 < nu)
        def _():
            start_tile(i + 1, 1 - slot)

        def wait(r, c):
            copy(i, r, slot).wait()
            return c
        lax.fori_loop(0, nv_ref[i], wait, 0)
        for c in range(n_chunks):
            v = stage[pl.ds(slot * slot_rows + c, tile, stride=pitch), :]
            o_ref[:, c * LANES:(c + 1) * LANES] = v.astype(BF16)


def _dispatch(xp, idx, n_used_tiles, n_valid, tile, d):
    n = idx.shape[0]
    nch = d // LANES
    pitch = _row_pitch(d)

    def blk(i, nu, nv, ix):
        return (jnp.minimum(i, nu[0] - 1), 0)

    return pl.pallas_call(
        functools.partial(_dispatch_kernel, tile, pitch, nch),
        grid_spec=pltpu.PrefetchScalarGridSpec(
            num_scalar_prefetch=3,
            grid=(n // tile,),
            in_specs=[pl.BlockSpec(memory_space=pl.ANY)],
            out_specs=pl.BlockSpec((tile, d), blk),
            scratch_shapes=[pltpu.VMEM((2 * tile * pitch, LANES), F32), pltpu.SemaphoreType.DMA((2,))],
        ),
        out_shape=SDS((n, d), BF16),
        compiler_params=_cparams(("arbitrary",)),
        name="moe_dispatch",
    )(n_used_tiles, n_valid, idx, xp)


def _weight_stream(be_ref, nu, parts, sem, b, f, nf):
    t = b * nf + f

    def copies(bb, ff, slot):
        e = be_ref[bb]
        return [pltpu.make_async_copy(src(e, ff), dst(slot), sem.at[slot, n])
                for n, (src, dst) in enumerate(parts)]

    def start(bb, ff, slot):
        for n, cp in enumerate(copies(bb, ff, slot)):
            cp.start(priority=n % 2)

    @pl.when(t == 0)
    def _():
        start(0, 0, 0)

    slot = t % 2
    t_next = t + 1
    b_next = t_next // nf

    @pl.when(b_next < nu)
    def _():
        start(b_next, t_next % nf, 1 - slot)

    for cp in copies(b, f, slot):
        cp.wait()
    return slot


def _moe_up_kernel(sub, nf, be_ref, nu_ref, nvb_ref, x_ref, wg_hbm, wu_hbm, hid_ref, wbuf, wgu_ref, sem):
    b = pl.program_id(0)
    f = pl.program_id(1)
    nu = nu_ref[0]
    tf = hid_ref.shape[-1]

    def col_chunk(w_hbm):
        return lambda e, ff: w_hbm.at[e, :, pl.ds(pl.multiple_of(ff * tf, tf), tf)]

    @pl.when(b < nu)
    def _():
        slot = _weight_stream(
            be_ref, nu,
            [(col_chunk(wg_hbm), lambda s: wbuf.at[s, 0]), (col_chunk(wu_hbm), lambda s: wbuf.at[s, 1])],
            sem, b, f, nf)
        wgu_ref[:, 0:tf] = wbuf[slot, 0].astype(BF16)
        wgu_ref[:, tf:2 * tf] = wbuf[slot, 1].astype(BF16)

        def sub_block(s, c):
            rs = pl.ds(pl.multiple_of(s * sub, sub), sub)
            gu = _dot(x_ref[rs, :], wgu_ref[...])
            hid_ref[rs, :] = (jax.nn.silu(gu[:, 0:tf]) * gu[:, tf:2 * tf]).astype(BF16)
            return c

        lax.fori_loop(0, (nvb_ref[b] + sub - 1) // sub, sub_block, 0)


def _moe_down_kernel(sub, pitch, nj, be_ref, nu_ref, nvb_ref, hid_ref, wd_hbm, yp_ref, wbuf, wd_bf_ref, sem):
    b = pl.program_id(0)
    j = pl.program_id(1)
    nu = nu_ref[0]
    de, tn = wd_bf_ref.shape
    half = de // 2

    def row_half(h):
        return (lambda e, jj: wd_hbm.at[e, pl.ds(h * half, half), pl.ds(pl.multiple_of(jj * tn, tn), tn)],
                lambda s: wbuf.at[s, pl.ds(h * half, half)])

    @pl.when(b < nu)
    def _():
        slot = _weight_stream(be_ref, nu, [row_half(0), row_half(1)], sem, b, j, nj)
        wd_bf_ref[...] = wbuf[slot].astype(BF16)

        def sub_block(s, c):
            r0 = pl.multiple_of(s * sub, sub)
            y = _dot(hid_ref[pl.ds(r0, sub), :], wd_bf_ref[...])
            _store_token_rows(yp_ref, y, r0 * pitch + j * (tn // LANES), pitch)
            return c

        lax.fori_loop(0, (nvb_ref[b] + sub - 1) // sub, sub_block, 0)


def _moe_ffn(x_sorted, blk_expert, n_used, n_valid_blk, w_gate, w_up, w_down, tm):
    a_pad, d = x_sorted.shape
    n_exp, _, de = w_gate.shape
    n_blocks = a_pad // tm
    sub = min(MOE_SUB, tm)

    def blk(b, nu):
        return jnp.minimum(b, nu[0] - 1)

    def frozen(b, f, nu, last):
        return jnp.where(b < nu[0], f, last)

    tf = _pick(de, 256, LANES)
    nf = de // tf
    hid = pl.pallas_call(
        functools.partial(_moe_up_kernel, sub, nf),
        grid_spec=pltpu.PrefetchScalarGridSpec(
            num_scalar_prefetch=3,
            grid=(n_blocks, nf),
            in_specs=[
                pl.BlockSpec((tm, d), lambda b, f, be, nu, nv: (blk(b, nu), 0)),
                pl.BlockSpec(memory_space=pl.ANY),
                pl.BlockSpec(memory_space=pl.ANY),
            ],
            out_specs=pl.BlockSpec((tm, tf), lambda b, f, be, nu, nv: (blk(b, nu), frozen(b, f, nu, nf - 1))),
            scratch_shapes=[pltpu.VMEM((2, 2, d, tf), F32), pltpu.VMEM((d, 2 * tf), BF16),
                            pltpu.SemaphoreType.DMA((2, 2))],
        ),
        out_shape=SDS((a_pad, de), BF16),
        compiler_params=_cparams(("arbitrary", "arbitrary")),
        name="moe_up",
    )(blk_expert, n_used, n_valid_blk, x_sorted, w_gate, w_up)

    tn = _pick(d, 1024, LANES)
    nj = d // tn
    pitch = _row_pitch(d)
    return pl.pallas_call(
        functools.partial(_moe_down_kernel, sub, pitch, nj),
        grid_spec=pltpu.PrefetchScalarGridSpec(
            num_scalar_prefetch=3,
            grid=(n_blocks, nj),
            in_specs=[
                pl.BlockSpec((tm, de), lambda b, j, be, nu, nv: (blk(b, nu), 0)),
                pl.BlockSpec(memory_space=pl.ANY),
            ],
            out_specs=pl.BlockSpec((tm * pitch, LANES), lambda b, j, be, nu, nv: (blk(b, nu), 0)),
            scratch_shapes=[pltpu.VMEM((2, de, tn), F32), pltpu.VMEM((de, tn), BF16),
                            pltpu.SemaphoreType.DMA((2, 2))],
        ),
        out_shape=SDS((a_pad * pitch, LANES), F32),
        compiler_params=_cparams(("arbitrary", "arbitrary")),
        name="moe_down",
    )(blk_expert, n_used, n_valid_blk, hid, w_down)


def _combine_kernel(tile, pitch, n_chunks, pos_ref, h_ref, wt_ref, yp_hbm, h2_ref, h2b_ref, ybuf, sem):
    i = pl.program_id(0)
    n_tiles = pl.num_programs(0)
    n_rows = TOP_K * tile
    slot_rows = n_rows * pitch

    def copy(t, r, slot):
        src = pl.multiple_of(pos_ref[t * n_rows + r] * pitch, SUBLANES)
        dst = pl.multiple_of(slot * slot_rows + r * pitch, SUBLANES)
        return pltpu.make_async_copy(
            yp_hbm.at[pl.ds(src, n_chunks)], ybuf.at[pl.ds(dst, n_chunks)], sem.at[slot])

    def start_tile(t, slot):
        def start(r2, c):
            copy(t, 2 * r2, slot).start(priority=0)
            copy(t, 2 * r2 + 1, slot).start(priority=1)
            return c
        lax.fori_loop(0, n_rows // 2, start, 0, unroll=4)

    @pl.when(i == 0)
    def _():
        start_tile(0, 0)

    slot = i % 2

    @pl.when(i + 1 < n_tiles)
    def _():
        start_tile(i + 1, 1 - slot)

    def wait(r, c):
        copy(i, r, slot).wait()
        return c
    lax.fori_loop(0, n_rows, wait, 0, unroll=8)
    wt = wt_ref[...]
    for c in range(n_chunks):
        cs = slice(c * LANES, (c + 1) * LANES)
        h2 = h_ref[:, cs]
        for kk in range(TOP_K):
            row0 = slot * slot_rows + kk * tile * pitch + c
            h2 = h2 + wt[:, kk:kk + 1] * ybuf[pl.ds(row0, tile, stride=pitch), :]
        h2_ref[:, cs] = h2
        h2b_ref[:, cs] = h2.astype(BF16)


def _combine(h, yp, pos, wts):
    m, d = h.shape
    tile = _pick(m, 256, 16)
    n_tiles = m // tile
    pitch = _row_pitch(d)
    pos_tiled = pos.reshape(n_tiles, tile, TOP_K).transpose(0, 2, 1).reshape(-1)
    return pl.pallas_call(
        functools.partial(_combine_kernel, tile, pitch, d // LANES),
        grid_spec=pltpu.PrefetchScalarGridSpec(
            num_scalar_prefetch=1,
            grid=(n_tiles,),
            in_specs=[
                pl.BlockSpec((tile, d), lambda i, p: (i, 0)),
                pl.BlockSpec((tile, TOP_K), lambda i, p: (i, 0)),
                pl.BlockSpec(memory_space=pl.ANY),
            ],
            out_specs=[pl.BlockSpec((tile, d), lambda i, p: (i, 0)),
                       pl.BlockSpec((tile, d), lambda i, p: (i, 0))],
            scratch_shapes=[pltpu.VMEM((2 * TOP_K * tile * pitch, LANES), F32),
                            pltpu.SemaphoreType.DMA((2,))],
        ),
        out_shape=[SDS((m, d), F32), SDS((m, d), BF16)],
        compiler_params=_cparams(("arbitrary",)),
        name="moe_combine",
    )(pos_tiled, h, wts, yp)


def _ple_kernel(final_norm, nj, hb_ref, h_ref, p_ref, wg_ref, wp_ref, g_ref, o_ref, acc_ref):
    j = pl.program_id(1)
    gate = jax.nn.sigmoid(_dot(hb_ref[...], wg_ref[...]))
    proj = _dot(p_ref[...].astype(BF16), wp_ref[...])
    acc_ref[j] = h_ref[...] + gate * proj

    @pl.when(j == nj - 1)
    def _():
        tn = acc_ref.shape[-1]
        if final_norm:
            ssq = jnp.zeros((acc_ref.shape[1], 1), F32)
            for c in range(nj):
                a = acc_ref[c]
                ssq = ssq + jnp.sum(a * a, axis=-1, keepdims=True)
            rs = lax.rsqrt(ssq / (nj * tn) + EPS)
            for c in range(nj):
                o_ref[:, c * tn:(c + 1) * tn] = acc_ref[c] * rs * g_ref[:, c * tn:(c + 1) * tn]
        else:
            for c in range(nj):
                o_ref[:, c * tn:(c + 1) * tn] = acc_ref[c]


def _ple(h2b, h2, p, w_gate, w_proj, g, rows0, rows, final_norm):
    d = h2.shape[1]
    pd = p.shape[1]
    tm = _pick(math.gcd(rows, rows0) if rows0 else rows, 512, 16)
    tn = _pick(d, 512, LANES)
    nj = d // tn
    ro = rows0 // tm
    return pl.pallas_call(
        functools.partial(_ple_kernel, final_norm, nj),
        grid=(rows // tm, nj),
        in_specs=[
            pl.BlockSpec((tm, d), lambda i, j: (i + ro, 0)),
            pl.BlockSpec((tm, tn), lambda i, j: (i + ro, j)),
            pl.BlockSpec((tm, pd), lambda i, j: (i, 0)),
            pl.BlockSpec((d, tn), lambda i, j: (0, j)),
            pl.BlockSpec((pd, tn), lambda i, j: (0, j)),
            pl.BlockSpec((1, d), lambda i, j: (0, 0)),
        ],
        out_specs=pl.BlockSpec((tm, d), lambda i, j: (i, 0)),
        out_shape=SDS((rows, d), F32),
        scratch_shapes=[pltpu.VMEM((nj, tm, tn), F32)],
        compiler_params=_cparams(("arbitrary", "arbitrary")),
        name="ple_gate",
    )(h2b, h2, p, w_gate, w_proj, g)


def _route(logits, n_groups, epg, tm, tile):
    m = logits.shape[0]
    n_exp = n_groups * epg
    g_logits = logits[:, :n_groups]
    g_idx = jnp.argmax(g_logits, axis=-1)
    g_w = 1.0 / jnp.sum(jnp.exp(g_logits - jnp.max(g_logits, axis=-1, keepdims=True)), axis=-1)
    e_logits = logits[:, n_groups:n_groups + n_exp].reshape(m, n_groups, epg)
    in_group = (jnp.arange(n_groups)[None, :] == g_idx[:, None])[:, :, None]
    e_in = jnp.sum(jnp.where(in_group, e_logits, 0.0), axis=1)
    lane = jnp.arange(epg)[None, :]
    first = jnp.argmax(e_in, axis=-1)
    v1 = jnp.max(e_in, axis=-1)
    rest = jnp.where(lane == first[:, None], NEG_INF, e_in)
    second = jnp.argmax(rest, axis=-1)
    v2 = jnp.max(rest, axis=-1)
    w2 = jnp.exp(v2 - v1)
    e_w = jnp.stack([1.0 / (1.0 + w2), w2 / (1.0 + w2)], axis=-1)
    ids = g_idx[:, None] * epg + jnp.stack([first, second], axis=-1)
    wts = g_w[:, None] * e_w

    a = m * TOP_K
    e_flat = ids.reshape(-1).astype(jnp.int32)
    onehot = (e_flat[:, None] == jnp.arange(n_exp, dtype=jnp.int32)[None, :]).astype(jnp.int32)
    counts = jnp.sum(onehot, axis=0)
    rank = jnp.sum((jnp.cumsum(onehot, axis=0) - onehot) * onehot, axis=1)
    padded = (counts + tm - 1) // tm * tm
    pad_end = jnp.cumsum(padded)
    pad_start = pad_end - padded
    dest = (pad_start[e_flat] + rank).astype(jnp.int32)
    n_blocks = -(-a // tm) + n_exp
    tok = jnp.arange(a, dtype=jnp.int32) // TOP_K
    row_src = jnp.zeros((n_blocks * tm,), jnp.int32).at[dest].set(tok)
    n_used = (pad_end[-1] // tm).astype(jnp.int32)
    blk_id = jnp.arange(n_blocks, dtype=jnp.int32)
    blk_expert = jnp.minimum(jnp.searchsorted(pad_end, blk_id * tm, side="right"), n_exp - 1)
    last = blk_expert[jnp.maximum(n_used - 1, 0)]
    blk_expert = jnp.where(blk_id < n_used, blk_expert, last).astype(jnp.int32)
    tile_id = jnp.arange(n_blocks * tm // tile, dtype=jnp.int32)
    tile_e = blk_expert[tile_id * tile // tm]
    valid_end = (pad_start + counts)[tile_e]
    n_valid = jnp.clip(valid_end - tile_id * tile, 0, tile)
    n_valid = jnp.where(tile_id * tile < pad_end[-1], n_valid, 0).astype(jnp.int32)
    n_used_tiles = (n_used * (tm // tile)).reshape(1)
    n_valid_blk = jnp.sum(n_valid.reshape(n_blocks, tm // tile), axis=1).astype(jnp.int32)
    return (row_src, dest.reshape(m, TOP_K), wts, blk_expert, n_used.reshape(1), n_used_tiles, n_valid,
            n_valid_blk)


def kernel(x_prompt, x_sample, p_prompt, p_sample, cache_ckv, cache_krope, state_C, state_n, state_m, page_table, norm_mix, w_in, b_igate, b_fgate, mlstm_norm, q_a_norm, w_q_b, kv_a_norm, w_kv_b, w_out, norm_ffn, w_group, b_group, w_router, b_router, w_gate, w_up, w_down, w_ple_proj, w_ple_gate, norm_final):
    depth = w_in.shape[0]
    bp, seq, d = x_prompt.shape
    n_dec, ts, _ = x_sample.shape
    mp, ms = bp * seq, n_dec * ts
    m = mp + ms
    hm_heads = b_igate.shape[-1]
    dk, dv = state_n.shape[-1], state_C.shape[-1]
    ql, kvl, rope = q_a_norm.shape[-1], kv_a_norm.shape[-1], cache_krope.shape[-1]
    att_w = d - hm_heads * dv
    a_heads = (w_q_b.shape[-1] - w_kv_b.shape[-1] + att_w) // rope
    av = att_w // a_heads
    n_groups, n_exp = w_group.shape[-1], w_router.shape[-1]
    epg = n_exp // n_groups
    n_pages, page = page_table.shape[1], cache_ckv.shape[2]
    past = n_pages * page
    wq, wv = hm_heads * dk, hm_heads * dv
    assert rope == LANES // 2 and a_heads % 2 == 0 and w_q_b.shape[-1] == a_heads * (A_NOPE + rope)
    assert dv == 2 * dk and mp % ms == 0 and n_groups + n_exp <= LANES and 2 * hm_heads <= LANES // 2

    half = rope // 2
    inv = 1.0 / (ROPE_THETA ** (jnp.arange(half, dtype=F32) / half))
    pos = jnp.concatenate([jnp.tile(jnp.arange(seq), bp), jnp.tile(past + jnp.arange(ts), n_dec)])
    ang = pos.astype(F32)[:, None] * inv[None, :]
    cos_t = jnp.tile(jnp.cos(ang), (1, LANES // half))
    sin_t = jnp.tile(jnp.concatenate([-jnp.sin(ang), jnp.sin(ang)], axis=1), (1, LANES // rope))

    hp = x_prompt.reshape(mp, d)
    hs = x_sample.reshape(ms, d)
    outs = {k: [] for k in ("ckv_p", "kr_p", "C_p", "n_p", "m_p", "ckv_s", "kr_s", "C_s", "n_s", "m_s")}
    for l in range(depth):
        wl = w_in[l]
        o_aq = 2 * wq + 2 * wv + 2 * hm_heads
        o_c = o_aq + ql
        o_kr = o_c + kvl
        w_main = jnp.concatenate([wl[:, :2 * wq + 2 * wv], wl[:, o_aq:o_kr]], axis=1).astype(BF16)
        n_small = rope + 2 * hm_heads
        w_small = jnp.concatenate(
            [wl[:, o_kr:o_kr + rope], wl[:, 2 * wq + 2 * wv:o_aq], jnp.zeros((d, LANES - n_small), F32)],
            axis=1).astype(BF16)
        gate_bias = jnp.concatenate(
            [jnp.zeros((rope,), F32), b_igate[l], b_fgate[l], jnp.zeros((LANES - n_small,), F32)]).reshape(1, LANES)
        wqb = w_q_b[l].reshape(ql, a_heads, A_NOPE + rope)
        wq_re = jnp.concatenate(
            [wqb[:, :, :A_NOPE].reshape(ql, -1), wqb[:, :, A_NOPE:].reshape(ql, -1)], axis=1).astype(BF16)
        wkvb = w_kv_b[l].astype(BF16)
        wkv3 = w_kv_b[l].reshape(kvl, a_heads, A_NOPE + av)
        w_ukt = wkv3[:, :, :A_NOPE].transpose(1, 2, 0).astype(BF16)
        w_uv = wkv3[:, :, A_NOPE:].transpose(1, 0, 2).astype(BF16)
        w_o = w_out[l].astype(BF16)
        w_rg = jnp.concatenate(
            [w_group[l], w_router[l], jnp.zeros((d, LANES - n_groups - n_exp), F32)], axis=1)
        b_rg = jnp.concatenate(
            [b_group[l], b_router[l], jnp.zeros((LANES - n_groups - n_exp,), F32)]).reshape(1, LANES)
        w_pg = w_ple_gate[l].astype(BF16)
        w_pp = w_ple_proj[l].astype(BF16)
        g_mix = norm_mix[l].reshape(1, d)

        zz = _in_proj(hp, g_mix, w_main, w_small, m, 0, None)
        z, zs = _in_proj(hs, g_mix, w_main, w_small, m, mp, zz)
        q_hm, ckv, kr = _mla_q(z, zs, cos_t, sin_t, q_a_norm[l].reshape(1, ql), kv_a_norm[l].reshape(1, kvl),
                               wq_re, a_heads, ql, kvl, rope, 2 * wq + 2 * wv, 2 * wq + 2 * wv + ql)
        ckv_p, ckv_s, kr_p, kr_s = ckv[:mp], ckv[mp:], kr[:mp], kr[mp:]

        norm_m = mlstm_norm[l].reshape(1, wv)
        lp = _pick(seq, 256, 16)
        hm, c_p, n_p, m_p = _mlstm(z, zs, gate_bias, norm_m, bp, seq, lp, 1, 0, hm_heads, dk, dv, m, None, None)
        n_sub = 2 if n_dec % 2 == 0 else 1
        hm, c_s, n_s, m_s = _mlstm(z, zs, gate_bias, norm_m, n_dec, ts, ts, n_sub, mp, hm_heads, dk, dv, m,
                                   (state_C[l], state_n[l], state_m[l].reshape(n_dec, 1, hm_heads)), hm)

        k_hm, v_hm = _kv_expand(ckv_p, kr_p, wkvb, mp, a_heads, av)
        att = _flash_prompt(q_hm, k_hm, v_hm, bp, seq, m)
        qs = _q_latent(q_hm, w_ukt, mp, n_dec, ts)
        o_s = _mla_sample(page_table, qs, ckv_s, kr_s, cache_ckv[l], jnp.swapaxes(cache_krope[l], 1, 2),
                          (A_NOPE + rope) ** -0.5)
        att = _o_uv(o_s, w_uv, att, mp)

        h1 = _out_proj(hm, att, w_o, hp, 0, m, None)
        h1 = _out_proj(hm, att, w_o, hs, mp, m, h1)

        xn2p, logits = _router(h1, norm_ffn[l].reshape(1, d), w_rg, b_rg)
        row_src, posn, wts, blk_expert, n_used, n_used_tiles, n_valid, n_valid_blk = _route(
            logits, n_groups, epg, MOE_BLOCK, MOE_TILE)
        x_sorted = _dispatch(xn2p, row_src, n_used_tiles, n_valid, MOE_TILE, d)
        yp = _moe_ffn(x_sorted, blk_expert, n_used, n_valid_blk, w_gate[l], w_up[l], w_down[l], MOE_BLOCK)
        h2, h2b = _combine(h1, yp, posn, wts)

        last = l == depth - 1
        g_fin = norm_final.reshape(1, d)
        hp = _ple(h2b, h2, p_prompt[l].reshape(mp, -1), w_pg, w_pp, g_fin, 0, mp, last)
        hs = _ple(h2b, h2, p_sample[l].reshape(ms, -1), w_pg, w_pp, g_fin, mp, ms, last)

        outs["ckv_p"].append(ckv_p.reshape(bp, seq, kvl))
        outs["kr_p"].append(kr_p.reshape(bp, seq, rope))
        outs["C_p"].append(c_p)
        outs["n_p"].append(n_p)
        outs["m_p"].append(m_p.reshape(bp, hm_heads))
        outs["ckv_s"].append(ckv_s.reshape(n_dec, ts, kvl))
        outs["kr_s"].append(kr_s.reshape(n_dec, ts, rope))
        outs["C_s"].append(c_s)
        outs["n_s"].append(n_s)
        outs["m_s"].append(m_s.reshape(n_dec, hm_heads))

    st = {k: jnp.stack(v) for k, v in outs.items()}
    return (hp.reshape(bp, seq, d), hs.reshape(n_dec, ts, d),
            st["ckv_p"], st["kr_p"], st["C_p"], st["n_p"], st["m_p"],
            st["ckv_s"], st["kr_s"], st["C_s"], st["n_s"], st["m_s"])
```

```python
import functools
import math

import jax
import jax.numpy as jnp
from jax import lax
from jax.experimental import pallas as pl
from jax.experimental.pallas import tpu as pltpu

F32 = jnp.float32
BF16 = jnp.bfloat16
SDS = jax.ShapeDtypeStruct

EPS = 1e-6
GATE_CAP = 15.0
ROPE_THETA = 10000.0
A_NOPE = 128
TOP_K = 2

LANES = 128
SUBLANES = 8
VMEM_LIMIT = 56 * 1024 * 1024
NEG_INF = float("-inf")

NT_DIMS = (((1,), (1,)), ((), ()))


def _cparams(sem, vmem=VMEM_LIMIT):
    return pltpu.CompilerParams(dimension_semantics=sem, vmem_limit_bytes=vmem)


def _pick(n, pref, mult):
    t = min(pref, n) // mult * mult
    while t >= mult:
        if n % t == 0:
            return t
        t -= mult
    return n


def _rup(x, m):
    return (x + m - 1) // m * m


def _dot(a, b):
    return jnp.dot(a, b, preferred_element_type=F32)


def _dot_nt(a, b, precision=None):
    return lax.dot_general(a, b, NT_DIMS, preferred_element_type=F32, precision=precision)


def _rms(x, g):
    ms = jnp.mean(x * x, axis=-1, keepdims=True)
    return x * lax.rsqrt(ms + EPS) * g


def _inproj_kernel(x_ref, g_ref, w_ref, ws_ref, *rest):
    z_ref, zs_ref, xn_ref = rest[-3:]
    j = pl.program_id(1)

    @pl.when(j == 0)
    def _():
        xn = _rms(x_ref[...], g_ref[...]).astype(BF16)
        xn_ref[...] = xn
        zs_ref[...] = _dot(xn, ws_ref[...])

    z_ref[...] = _dot(xn_ref[...], w_ref[...])


def _in_proj(x, g, w_main, w_small, m_total, row0, prev):
    rows, d = x.shape
    nm = w_main.shape[1]
    tm = _pick(math.gcd(rows, row0) if row0 else rows, 512, 16)
    tn = _pick(nm, 512, LANES)
    ro = row0 // tm
    ins = [x, g, w_main, w_small]
    in_specs = [
        pl.BlockSpec((tm, d), lambda i, j: (i, 0)),
        pl.BlockSpec((1, d), lambda i, j: (0, 0)),
        pl.BlockSpec((d, tn), lambda i, j: (0, j)),
        pl.BlockSpec((d, LANES), lambda i, j: (0, 0)),
    ]
    aliases = {}
    if prev is not None:
        ins += list(prev)
        in_specs += [pl.BlockSpec(memory_space=pl.ANY)] * 2
        aliases = {4: 0, 5: 1}
    return pl.pallas_call(
        _inproj_kernel,
        grid=(rows // tm, nm // tn),
        in_specs=in_specs,
        out_specs=[pl.BlockSpec((tm, tn), lambda i, j: (i + ro, j)),
                   pl.BlockSpec((tm, LANES), lambda i, j: (i + ro, 0))],
        out_shape=[SDS((m_total, nm), F32), SDS((m_total, LANES), F32)],
        scratch_shapes=[pltpu.VMEM((tm, d), BF16)],
        input_output_aliases=aliases,
        compiler_params=_cparams(("arbitrary", "arbitrary")),
        name="in_proj",
    )(*ins)


def _rope_lanes(y, cos, sin_signed, width):
    half = 32
    lane = lax.broadcasted_iota(jnp.int32, y.shape, 1)
    first = (lane % (2 * half)) < half
    swapped = jnp.where(first, pltpu.roll(y, width - half, 1), pltpu.roll(y, half, 1))
    return y * cos + swapped * sin_signed


def _mlaq_kernel(n_heads, rope, aq_ref, c_ref, zs_ref, cos_ref, sin_ref, qg_ref, kvg_ref, wq_ref,
                 q_ref, ckv_ref, kr_ref):
    aqn = _rms(aq_ref[...], qg_ref[...]).astype(BF16)
    y = _dot(aqn, wq_ref[...])
    cos = cos_ref[...]
    sin = sin_ref[...]
    nope_w = n_heads * A_NOPE
    for h2 in range(n_heads // 2):
        yr = y[:, nope_w + h2 * LANES: nope_w + (h2 + 1) * LANES]
        rot = _rope_lanes(yr, cos, sin, LANES)
        for s in range(2):
            h = 2 * h2 + s
            q_ref[h, :, 0:A_NOPE] = y[:, h * A_NOPE:(h + 1) * A_NOPE].astype(BF16)
            q_ref[h, :, A_NOPE:A_NOPE + rope] = rot[:, s * rope:(s + 1) * rope].astype(BF16)
    ckv_ref[...] = _rms(c_ref[...], kvg_ref[...])
    kr = _rope_lanes(zs_ref[...], cos, sin, LANES)
    kr_ref[...] = kr[:, 0:rope]


def _mla_q(z, zs, cos, sin, qg, kvg, wq, n_heads, ql, kvl, rope, aq_off, c_off):
    m = z.shape[0]
    tm = _pick(m, 256, 16)
    qd = A_NOPE + rope
    return pl.pallas_call(
        functools.partial(_mlaq_kernel, n_heads, rope),
        grid=(m // tm,),
        in_specs=[
            pl.BlockSpec((tm, ql), lambda i: (i, aq_off // ql)),
            pl.BlockSpec((tm, kvl), lambda i: (i, c_off // kvl)),
            pl.BlockSpec((tm, LANES), lambda i: (i, 0)),
            pl.BlockSpec((tm, LANES), lambda i: (i, 0)),
            pl.BlockSpec((tm, LANES), lambda i: (i, 0)),
            pl.BlockSpec((1, ql), lambda i: (0, 0)),
            pl.BlockSpec((1, kvl), lambda i: (0, 0)),
            pl.BlockSpec((ql, n_heads * qd), lambda i: (0, 0)),
        ],
        out_specs=[
            pl.BlockSpec((n_heads, tm, qd), lambda i: (0, i, 0)),
            pl.BlockSpec((tm, kvl), lambda i: (i, 0)),
            pl.BlockSpec((tm, rope), lambda i: (i, 0)),
        ],
        out_shape=[SDS((n_heads, m, qd), BF16), SDS((m, kvl), F32), SDS((m, rope), F32)],
        compiler_params=_cparams(("arbitrary",)),
        name="mla_q",
    )(z, z, zs, cos, sin, qg, kvg, wq)


def _kv_kernel(n_heads, av, rope, c_ref, kr_ref, w_ref, k_ref, v_ref):
    y = _dot(c_ref[...].astype(BF16), w_ref[...])
    kr = kr_ref[...].astype(BF16)
    hw = A_NOPE + av
    for h in range(n_heads):
        k_ref[h, :, 0:A_NOPE] = y[:, h * hw: h * hw + A_NOPE].astype(BF16)
        k_ref[h, :, A_NOPE:A_NOPE + rope] = kr
        v_ref[h] = y[:, h * hw + A_NOPE:(h + 1) * hw].astype(BF16)


def _kv_expand(ckv, kr, w_kvb, rows, n_heads, av):
    kvl = ckv.shape[1]
    rope = kr.shape[1]
    tm = _pick(rows, 256, 16)
    return pl.pallas_call(
        functools.partial(_kv_kernel, n_heads, av, rope),
        grid=(rows // tm,),
        in_specs=[
            pl.BlockSpec((tm, kvl), lambda i: (i, 0)),
            pl.BlockSpec((tm, rope), lambda i: (i, 0)),
            pl.BlockSpec(w_kvb.shape, lambda i: (0, 0)),
        ],
        out_specs=[
            pl.BlockSpec((n_heads, tm, A_NOPE + rope), lambda i: (0, i, 0)),
            pl.BlockSpec((n_heads, tm, av), lambda i: (0, i, 0)),
        ],
        out_shape=[SDS((n_heads, rows, A_NOPE + rope), BF16), SDS((n_heads, rows, av), BF16)],
        compiler_params=_cparams(("arbitrary",)),
        name="kv_expand",
    )(ckv, kr, w_kvb)


def _flash_kernel(tq, tk, scale, q_ref, k_ref, v_ref, o_ref):
    qi = pl.program_id(2)
    q = q_ref[0]
    av = v_ref.shape[-1]
    row = qi * tq + lax.broadcasted_iota(jnp.int32, (tq, tk), 0)
    col0 = lax.broadcasted_iota(jnp.int32, (tq, tk), 1)

    def body(masked, kb, carry):
        m, l, acc = carry
        start = pl.multiple_of(kb * tk, tk)
        k = k_ref[0, pl.ds(start, tk), :]
        v = v_ref[0, pl.ds(start, tk), :]
        s = _dot_nt(q, k) * scale
        if masked:
            s = jnp.where(col0 + kb * tk <= row, s, NEG_INF)
        m_new = jnp.maximum(m, jnp.max(s, axis=-1, keepdims=True))
        p = jnp.exp(s - m_new)
        alpha = jnp.exp(m - m_new)
        l = alpha * l + jnp.sum(p, axis=-1, keepdims=True)
        acc = alpha * acc + _dot(p.astype(BF16), v)
        return m_new, l, acc

    n_full = (qi * tq) // tk
    n_kb = (qi * tq + tq + tk - 1) // tk
    init = (jnp.full((tq, 1), NEG_INF, F32), jnp.zeros((tq, 1), F32), jnp.zeros((tq, av), F32))
    carry = lax.fori_loop(0, n_full, functools.partial(body, False), init)
    _, l, acc = lax.fori_loop(n_full, n_kb, functools.partial(body, True), carry)
    o_ref[...] = (acc / l).astype(o_ref.dtype)


def _flash_prompt(q_hm, k_hm, v_hm, n_batch, seq, m_total):
    n_heads, _, qd = q_hm.shape
    av = v_hm.shape[-1]
    tq = _pick(seq, 512, 16)
    tk = _pick(seq, 512, 16)
    nq = seq // tq
    return pl.pallas_call(
        functools.partial(_flash_kernel, tq, tk, qd ** -0.5),
        grid=(n_batch, n_heads, nq),
        in_specs=[
            pl.BlockSpec((1, tq, qd), lambda b, h, i: (h, b * nq + i, 0)),
            pl.BlockSpec((1, seq, qd), lambda b, h, i: (h, b, 0)),
            pl.BlockSpec((1, seq, av), lambda b, h, i: (h, b, 0)),
        ],
        out_specs=pl.BlockSpec((tq, av), lambda b, h, i: (b * nq + i, h)),
        out_shape=SDS((m_total, n_heads * av), BF16),
        compiler_params=_cparams(("arbitrary", "arbitrary", "arbitrary")),
        name="flash_prompt",
    )(q_hm, k_hm, v_hm)


def _qlat_kernel(n_seq, ts, kvl, rope, q_ref, w_ref, o_ref):
    q = q_ref[0]
    qlat = _dot(q[:, 0:A_NOPE], w_ref[0])
    width = o_ref.shape[-1]
    full = jnp.concatenate(
        [qlat, q[:, A_NOPE:A_NOPE + rope].astype(F32),
         jnp.zeros((q.shape[0], width - kvl - rope), F32)], axis=1)
    o_ref[...] = full.reshape(n_seq, 1, ts, width)


def _q_latent(q_hm, w_ukt, rows0, n_seq, ts):
    n_heads, _, qd = q_hm.shape
    kvl = w_ukt.shape[-1]
    rope = qd - A_NOPE
    ms = n_seq * ts
    width = _rup(kvl + rope, LANES)
    return pl.pallas_call(
        functools.partial(_qlat_kernel, n_seq, ts, kvl, rope),
        grid=(n_heads,),
        in_specs=[
            pl.BlockSpec((1, ms, qd), lambda h: (h, rows0 // ms, 0)),
            pl.BlockSpec((1, A_NOPE, kvl), lambda h: (h, 0, 0)),
        ],
        out_specs=pl.BlockSpec((n_seq, 1, ts, width), lambda h: (0, h, 0, 0)),
        out_shape=SDS((n_seq, n_heads, ts, width), F32),
        compiler_params=_cparams(("arbitrary",)),
        name="q_latent",
    )(q_hm, w_ukt)


MLA_BUFFERS = 3
MLA_STREAMS = 4


def _mla_sample_kernel(n_pages, chunk_pages, page, n_seq, ts, kvl, rope, scale,
                       pt_ref, q_ref, cn_ref, kn_ref, ckv_hbm, krt_hbm, o_ref,
                       kbuf, rbuf, sem, *stats):
    m_ref, l_ref, acc_ref = stats[0::3], stats[1::3], stats[2::3]
    b = pl.program_id(0)
    n_chunks = n_pages // chunk_pages
    n_total = n_seq * n_chunks
    rows = q_ref.shape[1] * ts
    keys = chunk_pages * page
    part = keys // MLA_STREAMS

    def chunk_copies(g, slot):
        gw = g % n_total
        base = (gw // n_chunks) * n_pages + (gw % n_chunks) * chunk_pages
        copies = []
        for p in range(chunk_pages):
            pg = pt_ref[base + p]
            copies.append(pltpu.make_async_copy(
                ckv_hbm.at[pg], kbuf.at[slot, pl.ds(p * page, page)], sem.at[slot]))
            copies.append(pltpu.make_async_copy(
                krt_hbm.at[pg], rbuf.at[slot, :, pl.ds(p * page, page)], sem.at[slot]))
        return copies

    def start_chunk(g, slot):
        for n, cp in enumerate(chunk_copies(g, slot)):
            cp.start(priority=(n // 2 + n) % 2)

    @pl.when(b == 0)
    def _():
        for g0 in range(MLA_BUFFERS - 1):
            start_chunk(g0, g0)

    q = q_ref[0].reshape(rows, q_ref.shape[-1])
    ql = q[:, 0:kvl].astype(BF16)
    qr = q[:, kvl:kvl + rope].astype(BF16)
    for j in range(MLA_STREAMS):
        m_ref[j][...] = jnp.full(m_ref[j].shape, NEG_INF, F32)
        l_ref[j][...] = jnp.zeros(l_ref[j].shape, F32)
        acc_ref[j][...] = jnp.zeros(acc_ref[j].shape, F32)

    def softmax_step(j, s):
        m = m_ref[j][...]
        m_new = jnp.maximum(m, jnp.max(s, axis=-1, keepdims=True))
        p = jnp.exp(s - m_new)
        alpha = jnp.exp(m - m_new)
        l_ref[j][...] = alpha * l_ref[j][...] + jnp.sum(p, axis=-1, keepdims=True)
        m_ref[j][...] = m_new
        return p.astype(BF16), alpha

    def accumulate(j, p, alpha, values):
        acc_ref[j][...] = alpha * acc_ref[j][...] + _dot(p, values)

    def body(c, carry):
        g = b * n_chunks + c
        slot = g % MLA_BUFFERS
        ahead = g + MLA_BUFFERS - 1
        start_chunk(ahead, ahead % MLA_BUFFERS)
        for cp in chunk_copies(g, slot):
            cp.wait()
        ks = [kbuf[slot, j * part:(j + 1) * part, :].astype(BF16) for j in range(MLA_STREAMS)]
        ss = [(_dot_nt(ql, ks[j]) + _dot(qr, rbuf[slot, :, j * part:(j + 1) * part].astype(BF16))) * scale
              for j in range(MLA_STREAMS)]
        pa = [softmax_step(j, ss[j]) for j in range(MLA_STREAMS)]
        for j in range(MLA_STREAMS):
            accumulate(j, pa[j][0], pa[j][1], ks[j])
        return carry

    lax.fori_loop(0, n_chunks, body, 0)

    @pl.when(b == n_seq - 1)
    def _():
        for d in range(MLA_BUFFERS - 1):
            for cp in chunk_copies(n_total + d, (n_total + d) % MLA_BUFFERS):
                cp.wait()

    pad = 2 * SUBLANES - ts
    cn = jnp.concatenate([cn_ref[...], jnp.zeros((pad, kvl), F32)], axis=0).astype(BF16)
    kn = jnp.concatenate([kn_ref[...], jnp.zeros((pad, rope), F32)], axis=0).astype(BF16)
    s = (_dot_nt(ql, cn) + _dot_nt(qr, kn)) * scale
    tok = lax.broadcasted_iota(jnp.int32, s.shape, 0) % ts
    col = lax.broadcasted_iota(jnp.int32, s.shape, 1)
    s = jnp.where(col <= tok, s, NEG_INF)
    p_new, alpha_new = softmax_step(0, s)
    accumulate(0, p_new, alpha_new, cn)

    m_all = m_ref[0][...]
    for j in range(1, MLA_STREAMS):
        m_all = jnp.maximum(m_all, m_ref[j][...])
    l_all = jnp.zeros(m_all.shape, F32)
    acc = jnp.zeros(acc_ref[0].shape, F32)
    for j in range(MLA_STREAMS):
        w = jnp.exp(m_ref[j][...] - m_all)
        l_all = l_all + w * l_ref[j][...]
        acc = acc + w * acc_ref[j][...]
    o_ref[0] = (acc / l_all).reshape(o_ref.shape[1:])


def _mla_sample(page_table, qs, ckv_s, kr_s, cache_ckv, cache_krt, scale):
    n_seq, n_heads, ts, width = qs.shape
    n_pages = page_table.shape[1]
    page, kvl = cache_ckv.shape[1:]
    rope = cache_krt.shape[1]
    chunk_pages = _pick(n_pages, 32, 1)
    rows = n_heads * ts
    keys = chunk_pages * page
    assert n_seq * (n_pages // chunk_pages) >= MLA_BUFFERS and keys % (MLA_STREAMS * LANES) == 0
    kern = functools.partial(_mla_sample_kernel, n_pages, chunk_pages, page, n_seq, ts, kvl, rope, scale)
    return pl.pallas_call(
        kern,
        grid_spec=pltpu.PrefetchScalarGridSpec(
            num_scalar_prefetch=1,
            grid=(n_seq,),
            in_specs=[
                pl.BlockSpec((1, n_heads, ts, width), lambda b, pt: (b, 0, 0, 0)),
                pl.BlockSpec((ts, kvl), lambda b, pt: (b, 0)),
                pl.BlockSpec((ts, rope), lambda b, pt: (b, 0)),
                pl.BlockSpec(memory_space=pl.ANY),
                pl.BlockSpec(memory_space=pl.ANY),
            ],
            out_specs=pl.BlockSpec((1, n_heads, ts, kvl), lambda b, pt: (b, 0, 0, 0)),
            scratch_shapes=[
                pltpu.VMEM((MLA_BUFFERS, keys, kvl), F32),
                pltpu.VMEM((MLA_BUFFERS, rope, keys), F32),
                pltpu.SemaphoreType.DMA((MLA_BUFFERS,)),
            ] + MLA_STREAMS * [pltpu.VMEM((rows, 1), F32), pltpu.VMEM((rows, 1), F32),
                               pltpu.VMEM((rows, kvl), F32)],
        ),
        out_shape=SDS((n_seq, n_heads, ts, kvl), F32),
        compiler_params=_cparams(("arbitrary",)),
        name="mla_sample",
    )(page_table.reshape(-1), qs, ckv_s, kr_s, cache_ckv, cache_krt)


def _ouv_kernel(o_ref, w_ref, prev_ref, att_ref):
    del prev_ref
    o = o_ref[...]
    o = o.reshape(o.shape[0] * o.shape[2], o.shape[3]).astype(BF16)
    att_ref[...] = _dot(o, w_ref[0]).astype(att_ref.dtype)


def _o_uv(o_s, w_uv, att, rows0):
    n_seq, n_heads, ts, kvl = o_s.shape
    av = w_uv.shape[-1]
    ms = n_seq * ts
    return pl.pallas_call(
        _ouv_kernel,
        grid=(n_heads,),
        in_specs=[
            pl.BlockSpec((n_seq, 1, ts, kvl), lambda h: (0, h, 0, 0)),
            pl.BlockSpec((1, kvl, av), lambda h: (h, 0, 0)),
            pl.BlockSpec(memory_space=pl.ANY),
        ],
        out_specs=pl.BlockSpec((ms, av), lambda h: (rows0 // ms, h)),
        out_shape=SDS(att.shape, att.dtype),
        input_output_aliases={2: 0},
        compiler_params=_cparams(("arbitrary",)),
        name="o_uv",
    )(o_s, w_uv, att)


def _mlstm_kernel(n_sub, L, n_heads, dk, dv, has_state, *refs):
    if has_state:
        (q_ref, k_ref, v_ref, mo_ref, zs_ref, bias_ref, norm_ref, c0_ref, n0_ref, m0_ref, _prev,
         hm_ref, c_ref, n_ref, m_ref) = refs
    else:
        (q_ref, k_ref, v_ref, mo_ref, zs_ref, bias_ref, norm_ref,
         hm_ref, c_ref, n_ref, m_ref) = refs
    ci = pl.program_id(1)
    Lk = _rup(L, LANES)

    @pl.when(ci == 0)
    def _():
        if has_state:
            c_ref[...] = c0_ref[...]
            n_ref[...] = n0_ref[...]
            m_ref[...] = m0_ref[...]
        else:
            c_ref[...] = jnp.zeros(c_ref.shape, F32)
            n_ref[...] = jnp.zeros(n_ref.shape, F32)
            m_ref[...] = jnp.zeros(m_ref.shape, F32)

    def pad_rows(a):
        if Lk == L:
            return a
        return jnp.concatenate([a, jnp.zeros((Lk - L, a.shape[1]), a.dtype)], axis=0)

    row = lax.broadcasted_iota(jnp.int32, (L, Lk), 0)
    col = lax.broadcasted_iota(jnp.int32, (L, Lk), 1)
    causal = col <= row
    tril = causal.astype(F32)
    sel_r = lax.broadcasted_iota(jnp.int32, (SUBLANES, LANES), 0)
    sel_c = lax.broadcasted_iota(jnp.int32, (SUBLANES, LANES), 1)
    lane0 = LANES // 2
    sel = ((sel_c == sel_r + lane0) & (sel_r < n_heads)).astype(F32)
    eye = (lax.broadcasted_iota(jnp.int32, (dk, dk), 0)
           == lax.broadcasted_iota(jnp.int32, (dk, dk), 1)).astype(BF16)
    hi = lax.Precision.HIGHEST

    for sq in range(n_sub):
        rs = slice(sq * L, (sq + 1) * L)
        pre = zs_ref[rs, :] + bias_ref[...]
        cap = GATE_CAP * jnp.tanh(pre / GATE_CAP)
        i_all = cap
        f_all = jnp.minimum(cap, 0.0) - jnp.log1p(jnp.exp(-jnp.abs(cap)))
        bcum_all = jnp.dot(tril, pad_rows(f_all), preferred_element_type=F32, precision=hi)
        r_all = i_all - pltpu.roll(bcum_all, LANES - n_heads, 1)
        r_rows = _dot_nt(sel, pad_rows(r_all), precision=hi)
        for h in range(n_heads):
            i_col = i_all[:, lane0 + h: lane0 + h + 1]
            b_col = bcum_all[:, lane0 + n_heads + h: lane0 + n_heads + h + 1]
            r_row = r_rows[h:h + 1, :]
            b_tot = b_col[L - 1:L, :]
            m_prev = m_ref[sq, :, h:h + 1]
            c_prev = c_ref[sq, h]
            n_prev = n_ref[sq, h:h + 1, :]

            q = q_ref[rs, h * dk:(h + 1) * dk] * (dk ** -0.5)
            k = k_ref[rs, h * dk:(h + 1) * dk]
            v = v_ref[rs, h * dv:(h + 1) * dv]
            qb = q.astype(BF16)
            kpb = pad_rows(k).astype(BF16)
            vpb = pad_rows(v).astype(BF16)

            dlog = jnp.where(causal, b_col + r_row, NEG_INF)
            inter = b_col + m_prev
            m_t = jnp.maximum(inter, jnp.max(dlog, axis=-1, keepdims=True))
            s = _dot_nt(qb, kpb) * jnp.exp(dlog - m_t)
            w_prev = jnp.exp(inter - m_t)
            num = _dot(s.astype(BF16), vpb) + w_prev * _dot(qb, c_prev.astype(BF16))
            den = (jnp.sum(s, axis=-1, keepdims=True)
                   + w_prev * jnp.sum(q * n_prev, axis=-1, keepdims=True))
            hout = num / jnp.maximum(jnp.abs(den), jnp.exp(-m_t))

            g = b_tot - b_col + i_col
            m_new = jnp.maximum(b_tot + m_prev, jnp.max(g, axis=0, keepdims=True))
            wk = jnp.exp(g - m_new)
            decay = jnp.exp(b_tot + m_prev - m_new)
            kw = wk * k
            kw_t = _dot_nt(eye, pad_rows(kw).astype(BF16)).astype(BF16)
            c_ref[sq, h] = decay * c_prev + _dot(kw_t, vpb)
            n_ref[sq, h:h + 1, :] = decay * n_prev + jnp.sum(kw, axis=0, keepdims=True)
            m_ref[sq, :, h:h + 1] = m_new

            hn = _rms(hout, norm_ref[:, h * dv:(h + 1) * dv])
            hn = hn * jax.nn.sigmoid(mo_ref[rs, h * dv:(h + 1) * dv])
            hm_ref[rs, h * dv:(h + 1) * dv] = hn.astype(hm_ref.dtype)


def _mlstm(z, zs, bias, norm, n_seq, seq, L, n_sub, rows0, n_heads, dk, dv, m_total, state, prev_hm):
    wq = n_heads * dk
    wv = n_heads * dv
    n_chunks = seq // L
    rb = n_sub * L
    has_state = state is not None
    if has_state:
        assert n_chunks == 1
    ro = rows0 // rb

    def rmap(cb):
        return lambda b, c: (ro + b * n_chunks + c, cb)

    ins = [z, z, z, z, zs, bias, norm]
    in_specs = [
        pl.BlockSpec((rb, wq), rmap(0)),
        pl.BlockSpec((rb, wq), rmap(1)),
        pl.BlockSpec((rb, wv), rmap(1)),
        pl.BlockSpec((rb, wv), rmap(2)),
        pl.BlockSpec((rb, LANES), rmap(0)),
        pl.BlockSpec((1, LANES), lambda b, c: (0, 0)),
        pl.BlockSpec((1, wv), lambda b, c: (0, 0)),
    ]
    c_spec = pl.BlockSpec((n_sub, n_heads, dk, dv), lambda b, c: (b, 0, 0, 0))
    n_spec = pl.BlockSpec((n_sub, n_heads, dk), lambda b, c: (b, 0, 0))
    m_spec = pl.BlockSpec((n_sub, 1, n_heads), lambda b, c: (b, 0, 0))
    aliases = {}
    if has_state:
        ins += [state[0], state[1], state[2], prev_hm]
        in_specs += [c_spec, n_spec, m_spec, pl.BlockSpec(memory_space=pl.ANY)]
        aliases = {10: 0}
    return pl.pallas_call(
        functools.partial(_mlstm_kernel, n_sub, L, n_heads, dk, dv, has_state),
        grid=(n_seq // n_sub, n_chunks),
        in_specs=in_specs,
        out_specs=[pl.BlockSpec((rb, wv), rmap(0)), c_spec, n_spec, m_spec],
        out_shape=[SDS((m_total, wv), BF16), SDS((n_seq, n_heads, dk, dv), F32),
                   SDS((n_seq, n_heads, dk), F32), SDS((n_seq, 1, n_heads), F32)],
        input_output_aliases=aliases,
        compiler_params=_cparams(("arbitrary", "arbitrary")),
        name="mlstm_state" if has_state else "mlstm_prompt",
    )(*ins)


def _outproj_kernel(hm_ref, att_ref, w1_ref, w2_ref, x_ref, *rest):
    o_ref = rest[-1]
    o_ref[...] = x_ref[...] + _dot(hm_ref[...], w1_ref[...]) + _dot(att_ref[...], w2_ref[...])


def _out_proj(hm, att, w, x, rows0, m_total, prev):
    rows, d = x.shape
    k1, k2 = hm.shape[1], att.shape[1]
    assert k1 == k2 and w.shape[0] == k1 + k2
    tm = _pick(math.gcd(rows, rows0) if rows0 else rows, 1024, 16)
    tn = _pick(d, 512, LANES)
    ro = rows0 // tm
    ins = [hm, att, w, w, x]
    in_specs = [
        pl.BlockSpec((tm, k1), lambda i, j: (i + ro, 0)),
        pl.BlockSpec((tm, k2), lambda i, j: (i + ro, 0)),
        pl.BlockSpec((k1, tn), lambda i, j: (0, j)),
        pl.BlockSpec((k2, tn), lambda i, j: (1, j)),
        pl.BlockSpec((tm, tn), lambda i, j: (i, j)),
    ]
    aliases = {}
    if prev is not None:
        ins.append(prev)
        in_specs.append(pl.BlockSpec(memory_space=pl.ANY))
        aliases = {5: 0}
    return pl.pallas_call(
        _outproj_kernel,
        grid=(rows // tm, d // tn),
        in_specs=in_specs,
        out_specs=pl.BlockSpec((tm, tn), lambda i, j: (i + ro, j)),
        out_shape=SDS((m_total, d), F32),
        input_output_aliases=aliases,
        compiler_params=_cparams(("arbitrary", "arbitrary")),
        name="out_proj",
    )(*ins)


MOE_BLOCK = 512
MOE_SUB = 256
MOE_TILE = 256


def _row_pitch(d):
    return d // LANES + SUBLANES


def _store_token_rows(ref2d, val, row0, pitch):
    tm = val.shape[0]
    for cc in range(val.shape[1] // LANES):
        ref2d[pl.ds(row0 + cc, tm, stride=pitch), :] = val[:, cc * LANES:(cc + 1) * LANES]


def _router_kernel(pitch, h_ref, g_ref, w_ref, b_ref, xp_ref, lg_ref):
    xn = _rms(h_ref[...], g_ref[...])
    _store_token_rows(xp_ref, xn, 0, pitch)
    lg_ref[...] = jnp.dot(xn, w_ref[...], preferred_element_type=F32,
                          precision=lax.Precision.HIGHEST) + b_ref[...]


def _router(h, g, w_rg, b_rg):
    m, d = h.shape
    tm = _pick(m, 256, 16)
    pitch = _row_pitch(d)
    return pl.pallas_call(
        functools.partial(_router_kernel, pitch),
        grid=(m // tm,),
        in_specs=[
            pl.BlockSpec((tm, d), lambda i: (i, 0)),
            pl.BlockSpec((1, d), lambda i: (0, 0)),
            pl.BlockSpec((d, LANES), lambda i: (0, 0)),
            pl.BlockSpec((1, LANES), lambda i: (0, 0)),
        ],
        out_specs=[pl.BlockSpec((tm * pitch, LANES), lambda i: (i, 0)),
                   pl.BlockSpec((tm, LANES), lambda i: (i, 0))],
        out_shape=[SDS((m * pitch, LANES), F32), SDS((m, LANES), F32)],
        compiler_params=_cparams(("arbitrary",)),
        name="ffn_router",
    )(h, g, w_rg, b_rg)


def _dispatch_kernel(tile, pitch, n_chunks, nu_ref, nv_ref, idx_ref, xp_hbm, o_ref, stage, sem):
    i = pl.program_id(0)
    nu = nu_ref[0]
    slot_rows = tile * pitch

    def copy(t, r, slot):
        src = pl.multiple_of(idx_ref[t * tile + r] * pitch, SUBLANES)
        dst = pl.multiple_of(slot * slot_rows + r * pitch, SUBLANES)
        return pltpu.make_async_copy(
            xp_hbm.at[pl.ds(src, n_chunks)], stage.at[pl.ds(dst, n_chunks)], sem.at[slot])

    def start_tile(t, slot):
        nv = nv_ref[t]

        def start(r2, c):
            copy(t, 2 * r2, slot).start(priority=0)
            copy(t, 2 * r2 + 1, slot).start(priority=1)
            return c
        lax.fori_loop(0, nv // 2, start, 0)

        @pl.when(nv % 2 == 1)
        def _():
            copy(t, nv - 1, slot).start(priority=0)

    @pl.when(i == 0)
    def _():
        stage[...] = jnp.zeros(stage.shape, F32)
        start_tile(0, 0)

    @pl.when(i < nu)
    def _():
        slot = i % 2

        @pl.when(i + 1 < nu)
        def _():
            start_tile(i + 1, 1 - slot)

        def wait(r, c):
            copy(i, r, slot).wait()
            return c
        lax.fori_loop(0, nv_ref[i], wait, 0)
        for c in range(n_chunks):
            v = stage[pl.ds(slot * slot_rows + c, tile, stride=pitch), :]
            o_ref[:, c * LANES:(c + 1) * LANES] = v.astype(BF16)


def _dispatch(xp, idx, n_used_tiles, n_valid, tile, d):
    n = idx.shape[0]
    nch = d // LANES
    pitch = _row_pitch(d)

    def blk(i, nu, nv, ix):
        return (jnp.minimum(i, nu[0] - 1), 0)

    return pl.pallas_call(
        functools.partial(_dispatch_kernel, tile, pitch, nch),
        grid_spec=pltpu.PrefetchScalarGridSpec(
            num_scalar_prefetch=3,
            grid=(n // tile,),
            in_specs=[pl.BlockSpec(memory_space=pl.ANY)],
            out_specs=pl.BlockSpec((tile, d), blk),
            scratch_shapes=[pltpu.VMEM((2 * tile * pitch, LANES), F32), pltpu.SemaphoreType.DMA((2,))],
        ),
        out_shape=SDS((n, d), BF16),
        compiler_params=_cparams(("arbitrary",)),
        name="moe_dispatch",
    )(n_used_tiles, n_valid, idx, xp)


def _weight_stream(be_ref, nu, parts, sem, b, f, nf):
    t = b * nf + f

    def copies(bb, ff, slot):
        e = be_ref[bb]
        return [pltpu.make_async_copy(src(e, ff), dst(slot), sem.at[slot, n])
                for n, (src, dst) in enumerate(parts)]

    def start(bb, ff, slot):
        for n, cp in enumerate(copies(bb, ff, slot)):
            cp.start(priority=n % 2)

    @pl.when(t == 0)
    def _():
        start(0, 0, 0)

    slot = t % 2
    t_next = t + 1
    b_next = t_next // nf

    @pl.when(b_next < nu)
    def _():
        start(b_next, t_next % nf, 1 - slot)

    for cp in copies(b, f, slot):
        cp.wait()
    return slot


def _moe_up_kernel(sub, nf, be_ref, nu_ref, nvb_ref, x_ref, wg_hbm, wu_hbm, hid_ref, wbuf, sem):
    b = pl.program_id(0)
    f = pl.program_id(1)
    nu = nu_ref[0]
    tf = hid_ref.shape[-1]

    def col_chunk(w_hbm):
        return lambda e, ff: w_hbm.at[e, :, pl.ds(pl.multiple_of(ff * tf, tf), tf)]

    @pl.when(b < nu)
    def _():
        slot = _weight_stream(
            be_ref, nu,
            [(col_chunk(wg_hbm), lambda s: wbuf.at[s, 0]), (col_chunk(wu_hbm), lambda s: wbuf.at[s, 1])],
            sem, b, f, nf)
        w_gu = jnp.concatenate([wbuf[slot, 0].astype(BF16), wbuf[slot, 1].astype(BF16)], axis=1)

        def sub_block(s):
            rs = slice(s * sub, (s + 1) * sub)
            gu = _dot(x_ref[rs, :], w_gu)
            hid_ref[rs, :] = (jax.nn.silu(gu[:, 0:tf]) * gu[:, tf:2 * tf]).astype(BF16)

        sub_block(0)
        for s in range(1, x_ref.shape[0] // sub):
            pl.when(nvb_ref[b] > s * sub)(functools.partial(sub_block, s))


def _moe_down_kernel(sub, pitch, nj, be_ref, nu_ref, nvb_ref, hid_ref, wd_hbm, yp_ref, wbuf, sem):
    b = pl.program_id(0)
    j = pl.program_id(1)
    nu = nu_ref[0]
    _, de, tn = wbuf.shape
    half = de // 2

    def row_half(h):
        return (lambda e, jj: wd_hbm.at[e, pl.ds(h * half, half), pl.ds(pl.multiple_of(jj * tn, tn), tn)],
                lambda s: wbuf.at[s, pl.ds(h * half, half)])

    @pl.when(b < nu)
    def _():
        slot = _weight_stream(be_ref, nu, [row_half(0), row_half(1)], sem, b, j, nj)
        w_d = wbuf[slot].astype(BF16)

        def sub_block(s):
            y = _dot(hid_ref[s * sub:(s + 1) * sub, :], w_d)
            _store_token_rows(yp_ref, y, s * sub * pitch + j * (tn // LANES), pitch)

        sub_block(0)
        for s in range(1, hid_ref.shape[0] // sub):
            pl.when(nvb_ref[b] > s * sub)(functools.partial(sub_block, s))


def _moe_ffn(x_sorted, blk_expert, n_used, n_valid_blk, w_gate, w_up, w_down, tm):
    a_pad, d = x_sorted.shape
    n_exp, _, de = w_gate.shape
    n_blocks = a_pad // tm
    sub = min(MOE_SUB, tm)

    def blk(b, nu):
        return jnp.minimum(b, nu[0] - 1)

    def frozen(b, f, nu, last):
        return jnp.where(b < nu[0], f, last)

    tf = _pick(de, 256, LANES)
    nf = de // tf
    hid = pl.pallas_call(
        functools.partial(_moe_up_kernel, sub, nf),
        grid_spec=pltpu.PrefetchScalarGridSpec(
            num_scalar_prefetch=3,
            grid=(n_blocks, nf),
            in_specs=[
                pl.BlockSpec((tm, d), lambda b, f, be, nu, nv: (blk(b, nu), 0)),
                pl.BlockSpec(memory_space=pl.ANY),
                pl.BlockSpec(memory_space=pl.ANY),
            ],
            out_specs=pl.BlockSpec((tm, tf), lambda b, f, be, nu, nv: (blk(b, nu), frozen(b, f, nu, nf - 1))),
            scratch_shapes=[pltpu.VMEM((2, 2, d, tf), F32), pltpu.SemaphoreType.DMA((2, 2))],
        ),
        out_shape=SDS((a_pad, de), BF16),
        compiler_params=_cparams(("arbitrary", "arbitrary")),
        name="moe_up",
    )(blk_expert, n_used, n_valid_blk, x_sorted, w_gate, w_up)

    tn = _pick(d, 1024, LANES)
    nj = d // tn
    pitch = _row_pitch(d)
    return pl.pallas_call(
        functools.partial(_moe_down_kernel, sub, pitch, nj),
        grid_spec=pltpu.PrefetchScalarGridSpec(
            num_scalar_prefetch=3,
            grid=(n_blocks, nj),
            in_specs=[
                pl.BlockSpec((tm, de), lambda b, j, be, nu, nv: (blk(b, nu), 0)),
                pl.BlockSpec(memory_space=pl.ANY),
            ],
            out_specs=pl.BlockSpec((tm * pitch, LANES), lambda b, j, be, nu, nv: (blk(b, nu), 0)),
            scratch_shapes=[pltpu.VMEM((2, de, tn), F32), pltpu.SemaphoreType.DMA((2, 2))],
        ),
        out_shape=SDS((a_pad * pitch, LANES), F32),
        compiler_params=_cparams(("arbitrary", "arbitrary")),
        name="moe_down",
    )(blk_expert, n_used, n_valid_blk, hid, w_down)


def _combine_kernel(tile, pitch, n_chunks, pos_ref, h_ref, wt_ref, yp_hbm, h2_ref, h2b_ref, ybuf, sem):
    i = pl.program_id(0)
    n_tiles = pl.num_programs(0)
    n_rows = TOP_K * tile
    slot_rows = n_rows * pitch

    def copy(t, r, slot):
        src = pl.multiple_of(pos_ref[t * n_rows + r] * pitch, SUBLANES)
        dst = pl.multiple_of(slot * slot_rows + r * pitch, SUBLANES)
        return pltpu.make_async_copy(
            yp_hbm.at[pl.ds(src, n_chunks)], ybuf.at[pl.ds(dst, n_chunks)], sem.at[slot])

    def start_tile(t, slot):
        def start(r2, c):
            copy(t, 2 * r2, slot).start(priority=0)
            copy(t, 2 * r2 + 1, slot).start(priority=1)
            return c
        lax.fori_loop(0, n_rows // 2, start, 0, unroll=4)

    @pl.when(i == 0)
    def _():
        start_tile(0, 0)

    slot = i % 2

    @pl.when(i + 1 < n_tiles)
    def _():
        start_tile(i + 1, 1 - slot)

    def wait(r, c):
        copy(i, r, slot).wait()
        return c
    lax.fori_loop(0, n_rows, wait, 0, unroll=8)
    wt = wt_ref[...]
    for c in range(n_chunks):
        cs = slice(c * LANES, (c + 1) * LANES)
        h2 = h_ref[:, cs]
        for kk in range(TOP_K):
            row0 = slot * slot_rows + kk * tile * pitch + c
            h2 = h2 + wt[:, kk:kk + 1] * ybuf[pl.ds(row0, tile, stride=pitch), :]
        h2_ref[:, cs] = h2
        h2b_ref[:, cs] = h2.astype(BF16)


def _combine(h, yp, pos, wts):
    m, d = h.shape
    tile = _pick(m, 256, 16)
    n_tiles = m // tile
    pitch = _row_pitch(d)
    pos_tiled = pos.reshape(n_tiles, tile, TOP_K).transpose(0, 2, 1).reshape(-1)
    return pl.pallas_call(
        functools.partial(_combine_kernel, tile, pitch, d // LANES),
        grid_spec=pltpu.PrefetchScalarGridSpec(
            num_scalar_prefetch=1,
            grid=(n_tiles,),
            in_specs=[
                pl.BlockSpec((tile, d), lambda i, p: (i, 0)),
                pl.BlockSpec((tile, TOP_K), lambda i, p: (i, 0)),
                pl.BlockSpec(memory_space=pl.ANY),
            ],
            out_specs=[pl.BlockSpec((tile, d), lambda i, p: (i, 0)),
                       pl.BlockSpec((tile, d), lambda i, p: (i, 0))],
            scratch_shapes=[pltpu.VMEM((2 * TOP_K * tile * pitch, LANES), F32),
                            pltpu.SemaphoreType.DMA((2,))],
        ),
        out_shape=[SDS((m, d), F32), SDS((m, d), BF16)],
        compiler_params=_cparams(("arbitrary",)),
        name="moe_combine",
    )(pos_tiled, h, wts, yp)


def _ple_kernel(final_norm, nj, hb_ref, h_ref, p_ref, wg_ref, wp_ref, g_ref, o_ref, acc_ref):
    j = pl.program_id(1)
    gate = jax.nn.sigmoid(_dot(hb_ref[...], wg_ref[...]))
    proj = _dot(p_ref[...].astype(BF16), wp_ref[...])
    acc_ref[j] = h_ref[...] + gate * proj

    @pl.when(j == nj - 1)
    def _():
        tn = acc_ref.shape[-1]
        if final_norm:
            ssq = jnp.zeros((acc_ref.shape[1], 1), F32)
            for c in range(nj):
                a = acc_ref[c]
                ssq = ssq + jnp.sum(a * a, axis=-1, keepdims=True)
            rs = lax.rsqrt(ssq / (nj * tn) + EPS)
            for c in range(nj):
                o_ref[:, c * tn:(c + 1) * tn] = acc_ref[c] * rs * g_ref[:, c * tn:(c + 1) * tn]
        else:
            for c in range(nj):
                o_ref[:, c * tn:(c + 1) * tn] = acc_ref[c]


def _ple(h2b, h2, p, w_gate, w_proj, g, rows0, rows, final_norm):
    d = h2.shape[1]
    pd = p.shape[1]
    tm = _pick(math.gcd(rows, rows0) if rows0 else rows, 512, 16)
    tn = _pick(d, 512, LANES)
    nj = d // tn
    ro = rows0 // tm
    return pl.pallas_call(
        functools.partial(_ple_kernel, final_norm, nj),
        grid=(rows // tm, nj),
        in_specs=[
            pl.BlockSpec((tm, d), lambda i, j: (i + ro, 0)),
            pl.BlockSpec((tm, tn), lambda i, j: (i + ro, j)),
            pl.BlockSpec((tm, pd), lambda i, j: (i, 0)),
            pl.BlockSpec((d, tn), lambda i, j: (0, j)),
            pl.BlockSpec((pd, tn), lambda i, j: (0, j)),
            pl.BlockSpec((1, d), lambda i, j: (0, 0)),
        ],
        out_specs=pl.BlockSpec((tm, d), lambda i, j: (i, 0)),
        out_shape=SDS((rows, d), F32),
        scratch_shapes=[pltpu.VMEM((nj, tm, tn), F32)],
        compiler_params=_cparams(("arbitrary", "arbitrary")),
        name="ple_gate",
    )(h2b, h2, p, w_gate, w_proj, g)


def _route(logits, n_groups, epg, tm, tile):
    m = logits.shape[0]
    n_exp = n_groups * epg
    g_logits = logits[:, :n_groups]
    g_idx = jnp.argmax(g_logits, axis=-1)
    g_w = 1.0 / jnp.sum(jnp.exp(g_logits - jnp.max(g_logits, axis=-1, keepdims=True)), axis=-1)
    e_logits = logits[:, n_groups:n_groups + n_exp].reshape(m, n_groups, epg)
    in_group = (jnp.arange(n_groups)[None, :] == g_idx[:, None])[:, :, None]
    e_in = jnp.sum(jnp.where(in_group, e_logits, 0.0), axis=1)
    lane = jnp.arange(epg)[None, :]
    first = jnp.argmax(e_in, axis=-1)
    v1 = jnp.max(e_in, axis=-1)
    rest = jnp.where(lane == first[:, None], NEG_INF, e_in)
    second = jnp.argmax(rest, axis=-1)
    v2 = jnp.max(rest, axis=-1)
    w2 = jnp.exp(v2 - v1)
    e_w = jnp.stack([1.0 / (1.0 + w2), w2 / (1.0 + w2)], axis=-1)
    ids = g_idx[:, None] * epg + jnp.stack([first, second], axis=-1)
    wts = g_w[:, None] * e_w

    a = m * TOP_K
    e_flat = ids.reshape(-1).astype(jnp.int32)
    onehot = (e_flat[:, None] == jnp.arange(n_exp, dtype=jnp.int32)[None, :]).astype(jnp.int32)
    counts = jnp.sum(onehot, axis=0)
    rank = jnp.sum((jnp.cumsum(onehot, axis=0) - onehot) * onehot, axis=1)
    padded = (counts + tm - 1) // tm * tm
    pad_end = jnp.cumsum(padded)
    pad_start = pad_end - padded
    dest = (pad_start[e_flat] + rank).astype(jnp.int32)
    n_blocks = -(-a // tm) + n_exp
    tok = jnp.arange(a, dtype=jnp.int32) // TOP_K
    row_src = jnp.zeros((n_blocks * tm,), jnp.int32).at[dest].set(tok)
    n_used = (pad_end[-1] // tm).astype(jnp.int32)
    blk_id = jnp.arange(n_blocks, dtype=jnp.int32)
    blk_expert = jnp.minimum(jnp.searchsorted(pad_end, blk_id * tm, side="right"), n_exp - 1)
    last = blk_expert[jnp.maximum(n_used - 1, 0)]
    blk_expert = jnp.where(blk_id < n_used, blk_expert, last).astype(jnp.int32)
    tile_id = jnp.arange(n_blocks * tm // tile, dtype=jnp.int32)
    tile_e = blk_expert[tile_id * tile // tm]
    valid_end = (pad_start + counts)[tile_e]
    n_valid = jnp.clip(valid_end - tile_id * tile, 0, tile)
    n_valid = jnp.where(tile_id * tile < pad_end[-1], n_valid, 0).astype(jnp.int32)
    n_used_tiles = (n_used * (tm // tile)).reshape(1)
    n_valid_blk = jnp.sum(n_valid.reshape(n_blocks, tm // tile), axis=1).astype(jnp.int32)
    return (row_src, dest.reshape(m, TOP_K), wts, blk_expert, n_used.reshape(1), n_used_tiles, n_valid,
            n_valid_blk)


def kernel(x_prompt, x_sample, p_prompt, p_sample, cache_ckv, cache_krope, state_C, state_n, state_m, page_table, norm_mix, w_in, b_igate, b_fgate, mlstm_norm, q_a_norm, w_q_b, kv_a_norm, w_kv_b, w_out, norm_ffn, w_group, b_group, w_router, b_router, w_gate, w_up, w_down, w_ple_proj, w_ple_gate, norm_final):
    depth = w_in.shape[0]
    bp, seq, d = x_prompt.shape
    n_dec, ts, _ = x_sample.shape
    mp, ms = bp * seq, n_dec * ts
    m = mp + ms
    hm_heads = b_igate.shape[-1]
    dk, dv = state_n.shape[-1], state_C.shape[-1]
    ql, kvl, rope = q_a_norm.shape[-1], kv_a_norm.shape[-1], cache_krope.shape[-1]
    att_w = d - hm_heads * dv
    a_heads = (w_q_b.shape[-1] - w_kv_b.shape[-1] + att_w) // rope
    av = att_w // a_heads
    n_groups, n_exp = w_group.shape[-1], w_router.shape[-1]
    epg = n_exp // n_groups
    n_pages, page = page_table.shape[1], cache_ckv.shape[2]
    past = n_pages * page
    wq, wv = hm_heads * dk, hm_heads * dv
    assert rope == LANES // 2 and a_heads % 2 == 0 and w_q_b.shape[-1] == a_heads * (A_NOPE + rope)
    assert dv == 2 * dk and mp % ms == 0 and n_groups + n_exp <= LANES and 2 * hm_heads <= LANES // 2

    half = rope // 2
    inv = 1.0 / (ROPE_THETA ** (jnp.arange(half, dtype=F32) / half))
    pos = jnp.concatenate([jnp.tile(jnp.arange(seq), bp), jnp.tile(past + jnp.arange(ts), n_dec)])
    ang = pos.astype(F32)[:, None] * inv[None, :]
    cos_t = jnp.tile(jnp.cos(ang), (1, LANES // half))
    sin_t = jnp.tile(jnp.concatenate([-jnp.sin(ang), jnp.sin(ang)], axis=1), (1, LANES // rope))

    hp = x_prompt.reshape(mp, d)
    hs = x_sample.reshape(ms, d)
    outs = {k: [] for k in ("ckv_p", "kr_p", "C_p", "n_p", "m_p", "ckv_s", "kr_s", "C_s", "n_s", "m_s")}
    for l in range(depth):
        wl = w_in[l]
        o_aq = 2 * wq + 2 * wv + 2 * hm_heads
        o_c = o_aq + ql
        o_kr = o_c + kvl
        w_main = jnp.concatenate([wl[:, :2 * wq + 2 * wv], wl[:, o_aq:o_kr]], axis=1).astype(BF16)
        n_small = rope + 2 * hm_heads
        w_small = jnp.concatenate(
            [wl[:, o_kr:o_kr + rope], wl[:, 2 * wq + 2 * wv:o_aq], jnp.zeros((d, LANES - n_small), F32)],
            axis=1).astype(BF16)
        gate_bias = jnp.concatenate(
            [jnp.zeros((rope,), F32), b_igate[l], b_fgate[l], jnp.zeros((LANES - n_small,), F32)]).reshape(1, LANES)
        wqb = w_q_b[l].reshape(ql, a_heads, A_NOPE + rope)
        wq_re = jnp.concatenate(
            [wqb[:, :, :A_NOPE].reshape(ql, -1), wqb[:, :, A_NOPE:].reshape(ql, -1)], axis=1).astype(BF16)
        wkvb = w_kv_b[l].astype(BF16)
        wkv3 = w_kv_b[l].reshape(kvl, a_heads, A_NOPE + av)
        w_ukt = wkv3[:, :, :A_NOPE].transpose(1, 2, 0).astype(BF16)
        w_uv = wkv3[:, :, A_NOPE:].transpose(1, 0, 2).astype(BF16)
        w_o = w_out[l].astype(BF16)
        w_rg = jnp.concatenate(
            [w_group[l], w_router[l], jnp.zeros((d, LANES - n_groups - n_exp), F32)], axis=1)
        b_rg = jnp.concatenate(
            [b_group[l], b_router[l], jnp.zeros((LANES - n_groups - n_exp,), F32)]).reshape(1, LANES)
        w_pg = w_ple_gate[l].astype(BF16)
        w_pp = w_ple_proj[l].astype(BF16)
        g_mix = norm_mix[l].reshape(1, d)

        zz = _in_proj(hp, g_mix, w_main, w_small, m, 0, None)
        z, zs = _in_proj(hs, g_mix, w_main, w_small, m, mp, zz)
        q_hm, ckv, kr = _mla_q(z, zs, cos_t, sin_t, q_a_norm[l].reshape(1, ql), kv_a_norm[l].reshape(1, kvl),
                               wq_re, a_heads, ql, kvl, rope, 2 * wq + 2 * wv, 2 * wq + 2 * wv + ql)
        ckv_p, ckv_s, kr_p, kr_s = ckv[:mp], ckv[mp:], kr[:mp], kr[mp:]

        norm_m = mlstm_norm[l].reshape(1, wv)
        lp = _pick(seq, 256, 16)
        hm, c_p, n_p, m_p = _mlstm(z, zs, gate_bias, norm_m, bp, seq, lp, 1, 0, hm_heads, dk, dv, m, None, None)
        n_sub = 2 if n_dec % 2 == 0 else 1
        hm, c_s, n_s, m_s = _mlstm(z, zs, gate_bias, norm_m, n_dec, ts, ts, n_sub, mp, hm_heads, dk, dv, m,
                                   (state_C[l], state_n[l], state_m[l].reshape(n_dec, 1, hm_heads)), hm)

        k_hm, v_hm = _kv_expand(ckv_p, kr_p, wkvb, mp, a_heads, av)
        att = _flash_prompt(q_hm, k_hm, v_hm, bp, seq, m)
        qs = _q_latent(q_hm, w_ukt, mp, n_dec, ts)
        o_s = _mla_sample(page_table, qs, ckv_s, kr_s, cache_ckv[l], jnp.swapaxes(cache_krope[l], 1, 2),
                          (A_NOPE + rope) ** -0.5)
        att = _o_uv(o_s, w_uv, att, mp)

        h1 = _out_proj(hm, att, w_o, hp, 0, m, None)
        h1 = _out_proj(hm, att, w_o, hs, mp, m, h1)

        xn2p, logits = _router(h1, norm_ffn[l].reshape(1, d), w_rg, b_rg)
        row_src, posn, wts, blk_expert, n_used, n_used_tiles, n_valid, n_valid_blk = _route(
            logits, n_groups, epg, MOE_BLOCK, MOE_TILE)
        x_sorted = _dispatch(xn2p, row_src, n_used_tiles, n_valid, MOE_TILE, d)
        yp = _moe_ffn(x_sorted, blk_expert, n_used, n_valid_blk, w_gate[l], w_up[l], w_down[l], MOE_BLOCK)
        h2, h2b = _combine(h1, yp, posn, wts)

        last = l == depth - 1
        g_fin = norm_final.reshape(1, d)
        hp = _ple(h2b, h2, p_prompt[l].reshape(mp, -1), w_pg, w_pp, g_fin, 0, mp, last)
        hs = _ple(h2b, h2, p_sample[l].reshape(ms, -1), w_pg, w_pp, g_fin, mp, ms, last)

        outs["ckv_p"].append(ckv_p.reshape(bp, seq, kvl))
        outs["kr_p"].append(kr_p.reshape(bp, seq, rope))
        outs["C_p"].append(c_p)
        outs["n_p"].append(n_p)
        outs["m_p"].append(m_p.reshape(bp, hm_heads))
        outs["ckv_s"].append(ckv_s.reshape(n_dec, ts, kvl))
        outs["kr_s"].append(kr_s.reshape(n_dec, ts, rope))
        outs["C_s"].append(c_s)
        outs["n_s"].append(n_s)
        outs["m_s"].append(m_s.reshape(n_dec, hm_heads))

    st = {k: jnp.stack(v) for k, v in outs.items()}
    return (hp.reshape(bp, seq, d), hs.reshape(n_dec, ts, d),
            st["ckv_p"], st["kr_p"], st["C_p"], st["n_p"], st["m_p"],
            st["ckv_s"], st["kr_s"], st["C_s"], st["n_s"], st["m_s"])
```

```python
import functools
import math

import jax
import jax.numpy as jnp
from jax import lax
from jax.experimental import pallas as pl
from jax.experimental.pallas import tpu as pltpu

F32 = jnp.float32
BF16 = jnp.bfloat16
SDS = jax.ShapeDtypeStruct

EPS = 1e-6
GATE_CAP = 15.0
ROPE_THETA = 10000.0
A_NOPE = 128
TOP_K = 2

LANES = 128
SUBLANES = 8
VMEM_LIMIT = 56 * 1024 * 1024
NEG_INF = float("-inf")

NT_DIMS = (((1,), (1,)), ((), ()))


def _cparams(sem, vmem=VMEM_LIMIT):
    return pltpu.CompilerParams(dimension_semantics=sem, vmem_limit_bytes=vmem)


def _pick(n, pref, mult):
    t = min(pref, n) // mult * mult
    while t >= mult:
        if n % t == 0:
            return t
        t -= mult
    return n


def _rup(x, m):
    return (x + m - 1) // m * m


def _dot(a, b):
    return jnp.dot(a, b, preferred_element_type=F32)


def _dot_nt(a, b, precision=None):
    return lax.dot_general(a, b, NT_DIMS, preferred_element_type=F32, precision=precision)


def _rms(x, g):
    ms = jnp.mean(x * x, axis=-1, keepdims=True)
    return x * lax.rsqrt(ms + EPS) * g


def _inproj_kernel(x_ref, g_ref, w_ref, ws_ref, *rest):
    z_ref, zs_ref, xn_ref = rest[-3:]
    j = pl.program_id(1)

    @pl.when(j == 0)
    def _():
        xn = _rms(x_ref[...], g_ref[...]).astype(BF16)
        xn_ref[...] = xn
        zs_ref[...] = _dot(xn, ws_ref[...])

    z_ref[...] = _dot(xn_ref[...], w_ref[...])


def _in_proj(x, g, w_main, w_small, m_total, row0, prev):
    rows, d = x.shape
    nm = w_main.shape[1]
    tm = _pick(math.gcd(rows, row0) if row0 else rows, 512, 16)
    tn = _pick(nm, 512, LANES)
    ro = row0 // tm
    ins = [x, g, w_main, w_small]
    in_specs = [
        pl.BlockSpec((tm, d), lambda i, j: (i, 0)),
        pl.BlockSpec((1, d), lambda i, j: (0, 0)),
        pl.BlockSpec((d, tn), lambda i, j: (0, j)),
        pl.BlockSpec((d, LANES), lambda i, j: (0, 0)),
    ]
    aliases = {}
    if prev is not None:
        ins += list(prev)
        in_specs += [pl.BlockSpec(memory_space=pl.ANY)] * 2
        aliases = {4: 0, 5: 1}
    return pl.pallas_call(
        _inproj_kernel,
        grid=(rows // tm, nm // tn),
        in_specs=in_specs,
        out_specs=[pl.BlockSpec((tm, tn), lambda i, j: (i + ro, j)),
                   pl.BlockSpec((tm, LANES), lambda i, j: (i + ro, 0))],
        out_shape=[SDS((m_total, nm), F32), SDS((m_total, LANES), F32)],
        scratch_shapes=[pltpu.VMEM((tm, d), BF16)],
        input_output_aliases=aliases,
        compiler_params=_cparams(("arbitrary", "arbitrary")),
        name="in_proj",
    )(*ins)


def _rope_lanes(y, cos, sin_signed, width):
    half = 32
    lane = lax.broadcasted_iota(jnp.int32, y.shape, 1)
    first = (lane % (2 * half)) < half
    swapped = jnp.where(first, pltpu.roll(y, width - half, 1), pltpu.roll(y, half, 1))
    return y * cos + swapped * sin_signed


def _mlaq_kernel(n_heads, rope, aq_ref, c_ref, zs_ref, cos_ref, sin_ref, qg_ref, kvg_ref, wq_ref,
                 q_ref, ckv_ref, kr_ref):
    aqn = _rms(aq_ref[...], qg_ref[...]).astype(BF16)
    q_scale = (A_NOPE + rope) ** -0.5 * math.log2(math.e)
    y = _dot(aqn, wq_ref[...]) * q_scale
    cos = cos_ref[...]
    sin = sin_ref[...]
    nope_w = n_heads * A_NOPE
    for h2 in range(n_heads // 2):
        yr = y[:, nope_w + h2 * LANES: nope_w + (h2 + 1) * LANES]
        rot = _rope_lanes(yr, cos, sin, LANES)
        for s in range(2):
            h = 2 * h2 + s
            q_ref[h, :, 0:A_NOPE] = y[:, h * A_NOPE:(h + 1) * A_NOPE].astype(BF16)
            q_ref[h, :, A_NOPE:A_NOPE + rope] = rot[:, s * rope:(s + 1) * rope].astype(BF16)
    ckv_ref[...] = _rms(c_ref[...], kvg_ref[...])
    kr = _rope_lanes(zs_ref[...], cos, sin, LANES)
    kr_ref[...] = kr[:, 0:rope]


def _mla_q(z, zs, cos, sin, qg, kvg, wq, n_heads, ql, kvl, rope, aq_off, c_off):
    m = z.shape[0]
    tm = _pick(m, 256, 16)
    qd = A_NOPE + rope
    return pl.pallas_call(
        functools.partial(_mlaq_kernel, n_heads, rope),
        grid=(m // tm,),
        in_specs=[
            pl.BlockSpec((tm, ql), lambda i: (i, aq_off // ql)),
            pl.BlockSpec((tm, kvl), lambda i: (i, c_off // kvl)),
            pl.BlockSpec((tm, LANES), lambda i: (i, 0)),
            pl.BlockSpec((tm, LANES), lambda i: (i, 0)),
            pl.BlockSpec((tm, LANES), lambda i: (i, 0)),
            pl.BlockSpec((1, ql), lambda i: (0, 0)),
            pl.BlockSpec((1, kvl), lambda i: (0, 0)),
            pl.BlockSpec((ql, n_heads * qd), lambda i: (0, 0)),
        ],
        out_specs=[
            pl.BlockSpec((n_heads, tm, qd), lambda i: (0, i, 0)),
            pl.BlockSpec((tm, kvl), lambda i: (i, 0)),
            pl.BlockSpec((tm, rope), lambda i: (i, 0)),
        ],
        out_shape=[SDS((n_heads, m, qd), BF16), SDS((m, kvl), F32), SDS((m, rope), F32)],
        compiler_params=_cparams(("arbitrary",)),
        name="mla_q",
    )(z, z, zs, cos, sin, qg, kvg, wq)


def _kv_kernel(n_heads, av, rope, c_ref, kr_ref, w_ref, k_ref, v_ref):
    y = _dot(c_ref[...].astype(BF16), w_ref[...])
    kr = kr_ref[...].astype(BF16)
    hw = A_NOPE + av
    for h in range(n_heads):
        k_ref[h, :, 0:A_NOPE] = y[:, h * hw: h * hw + A_NOPE].astype(BF16)
        k_ref[h, :, A_NOPE:A_NOPE + rope] = kr
        v_ref[h] = y[:, h * hw + A_NOPE:(h + 1) * hw].astype(BF16)


def _kv_expand(ckv, kr, w_kvb, rows, n_heads, av):
    kvl = ckv.shape[1]
    rope = kr.shape[1]
    tm = _pick(rows, 256, 16)
    return pl.pallas_call(
        functools.partial(_kv_kernel, n_heads, av, rope),
        grid=(rows // tm,),
        in_specs=[
            pl.BlockSpec((tm, kvl), lambda i: (i, 0)),
            pl.BlockSpec((tm, rope), lambda i: (i, 0)),
            pl.BlockSpec(w_kvb.shape, lambda i: (0, 0)),
        ],
        out_specs=[
            pl.BlockSpec((n_heads, tm, A_NOPE + rope), lambda i: (0, i, 0)),
            pl.BlockSpec((n_heads, tm, av), lambda i: (0, i, 0)),
        ],
        out_shape=[SDS((n_heads, rows, A_NOPE + rope), BF16), SDS((n_heads, rows, av), BF16)],
        compiler_params=_cparams(("arbitrary",)),
        name="kv_expand",
    )(ckv, kr, w_kvb)


def _flash_kernel(tq, tk, q_ref, k_ref, v_ref, o_ref):
    qi = pl.program_id(2)
    q = q_ref[0]
    av = v_ref.shape[-1]
    row = qi * tq + lax.broadcasted_iota(jnp.int32, (tq, tk), 0)
    col0 = lax.broadcasted_iota(jnp.int32, (tq, tk), 1)

    def body(masked, kb, carry):
        m, l, acc = carry
        start = pl.multiple_of(kb * tk, tk)
        k = k_ref[0, pl.ds(start, tk), :]
        v = v_ref[0, pl.ds(start, tk), :]
        s = _dot_nt(q, k)
        if masked:
            s = jnp.where(col0 + kb * tk <= row, s, NEG_INF)
        m_new = jnp.maximum(m, jnp.max(s, axis=-1, keepdims=True))
        p = jnp.exp2(s - m_new)
        alpha = jnp.exp2(m - m_new)
        l = alpha * l + jnp.sum(p, axis=-1, keepdims=True)
        acc = alpha * acc + _dot(p.astype(BF16), v)
        return m_new, l, acc

    n_full = (qi * tq) // tk
    n_kb = (qi * tq + tq + tk - 1) // tk
    init = (jnp.full((tq, 1), NEG_INF, F32), jnp.zeros((tq, 1), F32), jnp.zeros((tq, av), F32))
    carry = lax.fori_loop(0, n_full, functools.partial(body, False), init)
    _, l, acc = lax.fori_loop(n_full, n_kb, functools.partial(body, True), carry)
    o_ref[...] = (acc / l).astype(o_ref.dtype)


def _flash_prompt(q_hm, k_hm, v_hm, n_batch, seq, m_total):
    n_heads, _, qd = q_hm.shape
    av = v_hm.shape[-1]
    tq = _pick(seq, 512, 16)
    tk = _pick(seq, 512, 16)
    nq = seq // tq
    return pl.pallas_call(
        functools.partial(_flash_kernel, tq, tk),
        grid=(n_batch, n_heads, nq),
        in_specs=[
            pl.BlockSpec((1, tq, qd), lambda b, h, i: (h, b * nq + i, 0)),
            pl.BlockSpec((1, seq, qd), lambda b, h, i: (h, b, 0)),
            pl.BlockSpec((1, seq, av), lambda b, h, i: (h, b, 0)),
        ],
        out_specs=pl.BlockSpec((tq, av), lambda b, h, i: (b * nq + i, h)),
        out_shape=SDS((m_total, n_heads * av), BF16),
        compiler_params=_cparams(("arbitrary", "arbitrary", "arbitrary")),
        name="flash_prompt",
    )(q_hm, k_hm, v_hm)


def _qlat_kernel(n_seq, ts, kvl, rope, q_ref, w_ref, o_ref):
    q = q_ref[0]
    qlat = _dot(q[:, 0:A_NOPE], w_ref[0])
    width = o_ref.shape[-1]
    full = jnp.concatenate(
        [qlat, q[:, A_NOPE:A_NOPE + rope].astype(F32),
         jnp.zeros((q.shape[0], width - kvl - rope), F32)], axis=1)
    o_ref[...] = full.reshape(n_seq, 1, ts, width)


def _q_latent(q_hm, w_ukt, rows0, n_seq, ts):
    n_heads, _, qd = q_hm.shape
    kvl = w_ukt.shape[-1]
    rope = qd - A_NOPE
    ms = n_seq * ts
    width = _rup(kvl + rope, LANES)
    return pl.pallas_call(
        functools.partial(_qlat_kernel, n_seq, ts, kvl, rope),
        grid=(n_heads,),
        in_specs=[
            pl.BlockSpec((1, ms, qd), lambda h: (h, rows0 // ms, 0)),
            pl.BlockSpec((1, A_NOPE, kvl), lambda h: (h, 0, 0)),
        ],
        out_specs=pl.BlockSpec((n_seq, 1, ts, width), lambda h: (0, h, 0, 0)),
        out_shape=SDS((n_seq, n_heads, ts, width), F32),
        compiler_params=_cparams(("arbitrary",)),
        name="q_latent",
    )(q_hm, w_ukt)


MLA_BUFFERS = 3
MLA_STREAMS = 4


def _mla_sample_kernel(n_pages, chunk_pages, page, n_seq, ts, kvl, rope,
                       pt_ref, q_ref, cn_ref, kn_ref, ckv_hbm, krt_hbm, o_ref,
                       kbuf, rbuf, sem, *stats):
    m_ref, l_ref, acc_ref = stats[0::3], stats[1::3], stats[2::3]
    b = pl.program_id(0)
    n_chunks = n_pages // chunk_pages
    n_total = n_seq * n_chunks
    rows = q_ref.shape[1] * ts
    keys = chunk_pages * page
    part = keys // MLA_STREAMS

    def chunk_copies(g, slot):
        gw = g % n_total
        base = (gw // n_chunks) * n_pages + (gw % n_chunks) * chunk_pages
        copies = []
        for p in range(chunk_pages):
            pg = pt_ref[base + p]
            copies.append(pltpu.make_async_copy(
                ckv_hbm.at[pg], kbuf.at[slot, pl.ds(p * page, page)], sem.at[slot]))
            copies.append(pltpu.make_async_copy(
                krt_hbm.at[pg], rbuf.at[slot, :, pl.ds(p * page, page)], sem.at[slot]))
        return copies

    def start_chunk(g, slot):
        for n, cp in enumerate(chunk_copies(g, slot)):
            cp.start(priority=(n // 2 + n) % 2)

    @pl.when(b == 0)
    def _():
        for g0 in range(MLA_BUFFERS - 1):
            start_chunk(g0, g0)

    q = q_ref[0].reshape(rows, q_ref.shape[-1])
    ql = q[:, 0:kvl].astype(BF16)
    qr = q[:, kvl:kvl + rope].astype(BF16)
    for j in range(MLA_STREAMS):
        m_ref[j][...] = jnp.full(m_ref[j].shape, NEG_INF, F32)
        l_ref[j][...] = jnp.zeros(l_ref[j].shape, F32)
        acc_ref[j][...] = jnp.zeros(acc_ref[j].shape, F32)

    def softmax_step(j, s):
        m = m_ref[j][...]
        m_new = jnp.maximum(m, jnp.max(s, axis=-1, keepdims=True))
        p = jnp.exp2(s - m_new)
        alpha = jnp.exp2(m - m_new)
        l_ref[j][...] = alpha * l_ref[j][...] + jnp.sum(p, axis=-1, keepdims=True)
        m_ref[j][...] = m_new
        return p.astype(BF16), alpha

    def accumulate(j, p, alpha, values):
        acc_ref[j][...] = alpha * acc_ref[j][...] + _dot(p, values)

    def body(c, carry):
        g = b * n_chunks + c
        slot = g % MLA_BUFFERS
        ahead = g + MLA_BUFFERS - 1
        start_chunk(ahead, ahead % MLA_BUFFERS)
        for cp in chunk_copies(g, slot):
            cp.wait()
        ks = [kbuf[slot, j * part:(j + 1) * part, :].astype(BF16) for j in range(MLA_STREAMS)]
        ss = [_dot_nt(ql, ks[j]) + _dot(qr, rbuf[slot, :, j * part:(j + 1) * part].astype(BF16))
              for j in range(MLA_STREAMS)]
        pa = [softmax_step(j, ss[j]) for j in range(MLA_STREAMS)]
        for j in range(MLA_STREAMS):
            accumulate(j, pa[j][0], pa[j][1], ks[j])
        return carry

    lax.fori_loop(0, n_chunks, body, 0)

    @pl.when(b == n_seq - 1)
    def _():
        for d in range(MLA_BUFFERS - 1):
            for cp in chunk_copies(n_total + d, (n_total + d) % MLA_BUFFERS):
                cp.wait()

    pad = 2 * SUBLANES - ts
    cn = jnp.concatenate([cn_ref[...], jnp.zeros((pad, kvl), F32)], axis=0).astype(BF16)
    kn = jnp.concatenate([kn_ref[...], jnp.zeros((pad, rope), F32)], axis=0).astype(BF16)
    s = _dot_nt(ql, cn) + _dot_nt(qr, kn)
    tok = lax.broadcasted_iota(jnp.int32, s.shape, 0) % ts
    col = lax.broadcasted_iota(jnp.int32, s.shape, 1)
    s = jnp.where(col <= tok, s, NEG_INF)
    p_new, alpha_new = softmax_step(0, s)
    accumulate(0, p_new, alpha_new, cn)

    m_all = m_ref[0][...]
    for j in range(1, MLA_STREAMS):
        m_all = jnp.maximum(m_all, m_ref[j][...])
    l_all = jnp.zeros(m_all.shape, F32)
    acc = jnp.zeros(acc_ref[0].shape, F32)
    for j in range(MLA_STREAMS):
        w = jnp.exp2(m_ref[j][...] - m_all)
        l_all = l_all + w * l_ref[j][...]
        acc = acc + w * acc_ref[j][...]
    o_ref[0] = (acc / l_all).reshape(o_ref.shape[1:])


def _mla_sample(page_table, qs, ckv_s, kr_s, cache_ckv, cache_krt):
    n_seq, n_heads, ts, width = qs.shape
    n_pages = page_table.shape[1]
    page, kvl = cache_ckv.shape[1:]
    rope = cache_krt.shape[1]
    chunk_pages = _pick(n_pages, 32, 1)
    rows = n_heads * ts
    keys = chunk_pages * page
    assert n_seq * (n_pages // chunk_pages) >= MLA_BUFFERS and keys % (MLA_STREAMS * LANES) == 0
    kern = functools.partial(_mla_sample_kernel, n_pages, chunk_pages, page, n_seq, ts, kvl, rope)
    return pl.pallas_call(
        kern,
        grid_spec=pltpu.PrefetchScalarGridSpec(
            num_scalar_prefetch=1,
            grid=(n_seq,),
            in_specs=[
                pl.BlockSpec((1, n_heads, ts, width), lambda b, pt: (b, 0, 0, 0)),
                pl.BlockSpec((ts, kvl), lambda b, pt: (b, 0)),
                pl.BlockSpec((ts, rope), lambda b, pt: (b, 0)),
                pl.BlockSpec(memory_space=pl.ANY),
                pl.BlockSpec(memory_space=pl.ANY),
            ],
            out_specs=pl.BlockSpec((1, n_heads, ts, kvl), lambda b, pt: (b, 0, 0, 0)),
            scratch_shapes=[
                pltpu.VMEM((MLA_BUFFERS, keys, kvl), F32),
                pltpu.VMEM((MLA_BUFFERS, rope, keys), F32),
                pltpu.SemaphoreType.DMA((MLA_BUFFERS,)),
            ] + MLA_STREAMS * [pltpu.VMEM((rows, 1), F32), pltpu.VMEM((rows, 1), F32),
                               pltpu.VMEM((rows, kvl), F32)],
        ),
        out_shape=SDS((n_seq, n_heads, ts, kvl), F32),
        compiler_params=_cparams(("arbitrary",)),
        name="mla_sample",
    )(page_table.reshape(-1), qs, ckv_s, kr_s, cache_ckv, cache_krt)


def _ouv_kernel(o_ref, w_ref, prev_ref, att_ref):
    del prev_ref
    o = o_ref[...]
    o = o.reshape(o.shape[0] * o.shape[2], o.shape[3]).astype(BF16)
    att_ref[...] = _dot(o, w_ref[0]).astype(att_ref.dtype)


def _o_uv(o_s, w_uv, att, rows0):
    n_seq, n_heads, ts, kvl = o_s.shape
    av = w_uv.shape[-1]
    ms = n_seq * ts
    return pl.pallas_call(
        _ouv_kernel,
        grid=(n_heads,),
        in_specs=[
            pl.BlockSpec((n_seq, 1, ts, kvl), lambda h: (0, h, 0, 0)),
            pl.BlockSpec((1, kvl, av), lambda h: (h, 0, 0)),
            pl.BlockSpec(memory_space=pl.ANY),
        ],
        out_specs=pl.BlockSpec((ms, av), lambda h: (rows0 // ms, h)),
        out_shape=SDS(att.shape, att.dtype),
        input_output_aliases={2: 0},
        compiler_params=_cparams(("arbitrary",)),
        name="o_uv",
    )(o_s, w_uv, att)


def _mlstm_kernel(n_sub, L, n_heads, dk, dv, has_state, *refs):
    if has_state:
        (q_ref, k_ref, v_ref, mo_ref, zs_ref, bias_ref, norm_ref, c0_ref, n0_ref, m0_ref, _prev,
         hm_ref, c_ref, n_ref, m_ref) = refs
    else:
        (q_ref, k_ref, v_ref, mo_ref, zs_ref, bias_ref, norm_ref,
         hm_ref, c_ref, n_ref, m_ref) = refs
    ci = pl.program_id(1)
    Lk = _rup(L, LANES)

    @pl.when(ci == 0)
    def _():
        if has_state:
            c_ref[...] = c0_ref[...]
            n_ref[...] = n0_ref[...]
            m_ref[...] = m0_ref[...]
        else:
            c_ref[...] = jnp.zeros(c_ref.shape, F32)
            n_ref[...] = jnp.zeros(n_ref.shape, F32)
            m_ref[...] = jnp.zeros(m_ref.shape, F32)

    def pad_rows(a):
        if Lk == L:
            return a
        return jnp.concatenate([a, jnp.zeros((Lk - L, a.shape[1]), a.dtype)], axis=0)

    row = lax.broadcasted_iota(jnp.int32, (L, Lk), 0)
    col = lax.broadcasted_iota(jnp.int32, (L, Lk), 1)
    causal = col <= row
    tril = causal.astype(F32)
    sel_r = lax.broadcasted_iota(jnp.int32, (SUBLANES, LANES), 0)
    sel_c = lax.broadcasted_iota(jnp.int32, (SUBLANES, LANES), 1)
    lane0 = LANES // 2
    sel = ((sel_c == sel_r + lane0) & (sel_r < n_heads)).astype(F32)
    eye = (lax.broadcasted_iota(jnp.int32, (dk, dk), 0)
           == lax.broadcasted_iota(jnp.int32, (dk, dk), 1)).astype(BF16)
    hi = lax.Precision.HIGHEST

    for sq in range(n_sub):
        rs = slice(sq * L, (sq + 1) * L)
        pre = zs_ref[rs, :] + bias_ref[...]
        cap = GATE_CAP * jnp.tanh(pre / GATE_CAP)
        i_all = cap
        f_all = jnp.minimum(cap, 0.0) - jnp.log1p(jnp.exp(-jnp.abs(cap)))
        bcum_all = jnp.dot(tril, pad_rows(f_all), preferred_element_type=F32, precision=hi)
        r_all = i_all - pltpu.roll(bcum_all, LANES - n_heads, 1)
        r_rows = _dot_nt(sel, pad_rows(r_all), precision=hi)
        for h in range(n_heads):
            i_col = i_all[:, lane0 + h: lane0 + h + 1]
            b_col = bcum_all[:, lane0 + n_heads + h: lane0 + n_heads + h + 1]
            r_row = r_rows[h:h + 1, :]
            b_tot = b_col[L - 1:L, :]
            m_prev = m_ref[sq, :, h:h + 1]
            c_prev = c_ref[sq, h]
            n_prev = n_ref[sq, h:h + 1, :]

            q = q_ref[rs, h * dk:(h + 1) * dk] * (dk ** -0.5)
            k = k_ref[rs, h * dk:(h + 1) * dk]
            v = v_ref[rs, h * dv:(h + 1) * dv]
            qb = q.astype(BF16)
            kpb = pad_rows(k).astype(BF16)
            vpb = pad_rows(v).astype(BF16)

            dlog = jnp.where(causal, b_col + r_row, NEG_INF)
            inter = b_col + m_prev
            m_t = jnp.maximum(inter, jnp.max(dlog, axis=-1, keepdims=True))
            s = _dot_nt(qb, kpb) * jnp.exp(dlog - m_t)
            w_prev = jnp.exp(inter - m_t)
            num = _dot(s.astype(BF16), vpb) + w_prev * _dot(qb, c_prev.astype(BF16))
            den = (jnp.sum(s, axis=-1, keepdims=True)
                   + w_prev * jnp.sum(q * n_prev, axis=-1, keepdims=True))
            hout = num / jnp.maximum(jnp.abs(den), jnp.exp(-m_t))

            g = b_tot - b_col + i_col
            m_new = jnp.maximum(b_tot + m_prev, jnp.max(g, axis=0, keepdims=True))
            wk = jnp.exp(g - m_new)
            decay = jnp.exp(b_tot + m_prev - m_new)
            kw = wk * k
            kw_t = _dot_nt(eye, pad_rows(kw).astype(BF16)).astype(BF16)
            c_ref[sq, h] = decay * c_prev + _dot(kw_t, vpb)
            n_ref[sq, h:h + 1, :] = decay * n_prev + jnp.sum(kw, axis=0, keepdims=True)
            m_ref[sq, :, h:h + 1] = m_new

            hn = _rms(hout, norm_ref[:, h * dv:(h + 1) * dv])
            hn = hn * jax.nn.sigmoid(mo_ref[rs, h * dv:(h + 1) * dv])
            hm_ref[rs, h * dv:(h + 1) * dv] = hn.astype(hm_ref.dtype)


def _mlstm(z, zs, bias, norm, n_seq, seq, L, n_sub, rows0, n_heads, dk, dv, m_total, state, prev_hm):
    wq = n_heads * dk
    wv = n_heads * dv
    n_chunks = seq // L
    rb = n_sub * L
    has_state = state is not None
    if has_state:
        assert n_chunks == 1
    ro = rows0 // rb

    def rmap(cb):
        return lambda b, c: (ro + b * n_chunks + c, cb)

    ins = [z, z, z, z, zs, bias, norm]
    in_specs = [
        pl.BlockSpec((rb, wq), rmap(0)),
        pl.BlockSpec((rb, wq), rmap(1)),
        pl.BlockSpec((rb, wv), rmap(1)),
        pl.BlockSpec((rb, wv), rmap(2)),
        pl.BlockSpec((rb, LANES), rmap(0)),
        pl.BlockSpec((1, LANES), lambda b, c: (0, 0)),
        pl.BlockSpec((1, wv), lambda b, c: (0, 0)),
    ]
    c_spec = pl.BlockSpec((n_sub, n_heads, dk, dv), lambda b, c: (b, 0, 0, 0))
    n_spec = pl.BlockSpec((n_sub, n_heads, dk), lambda b, c: (b, 0, 0))
    m_spec = pl.BlockSpec((n_sub, 1, n_heads), lambda b, c: (b, 0, 0))
    aliases = {}
    if has_state:
        ins += [state[0], state[1], state[2], prev_hm]
        in_specs += [c_spec, n_spec, m_spec, pl.BlockSpec(memory_space=pl.ANY)]
        aliases = {10: 0}
    return pl.pallas_call(
        functools.partial(_mlstm_kernel, n_sub, L, n_heads, dk, dv, has_state),
        grid=(n_seq // n_sub, n_chunks),
        in_specs=in_specs,
        out_specs=[pl.BlockSpec((rb, wv), rmap(0)), c_spec, n_spec, m_spec],
        out_shape=[SDS((m_total, wv), BF16), SDS((n_seq, n_heads, dk, dv), F32),
                   SDS((n_seq, n_heads, dk), F32), SDS((n_seq, 1, n_heads), F32)],
        input_output_aliases=aliases,
        compiler_params=_cparams(("arbitrary", "arbitrary")),
        name="mlstm_state" if has_state else "mlstm_prompt",
    )(*ins)


def _outproj_kernel(hm_ref, att_ref, w1_ref, w2_ref, x_ref, *rest):
    o_ref = rest[-1]
    o_ref[...] = x_ref[...] + _dot(hm_ref[...], w1_ref[...]) + _dot(att_ref[...], w2_ref[...])


def _out_proj(hm, att, w, x, rows0, m_total, prev):
    rows, d = x.shape
    k1, k2 = hm.shape[1], att.shape[1]
    assert k1 == k2 and w.shape[0] == k1 + k2
    tm = _pick(math.gcd(rows, rows0) if rows0 else rows, 1024, 16)
    tn = _pick(d, 512, LANES)
    ro = rows0 // tm
    ins = [hm, att, w, w, x]
    in_specs = [
        pl.BlockSpec((tm, k1), lambda i, j: (i + ro, 0)),
        pl.BlockSpec((tm, k2), lambda i, j: (i + ro, 0)),
        pl.BlockSpec((k1, tn), lambda i, j: (0, j)),
        pl.BlockSpec((k2, tn), lambda i, j: (1, j)),
        pl.BlockSpec((tm, tn), lambda i, j: (i, j)),
    ]
    aliases = {}
    if prev is not None:
        ins.append(prev)
        in_specs.append(pl.BlockSpec(memory_space=pl.ANY))
        aliases = {5: 0}
    return pl.pallas_call(
        _outproj_kernel,
        grid=(rows // tm, d // tn),
        in_specs=in_specs,
        out_specs=pl.BlockSpec((tm, tn), lambda i, j: (i + ro, j)),
        out_shape=SDS((m_total, d), F32),
        input_output_aliases=aliases,
        compiler_params=_cparams(("arbitrary", "arbitrary")),
        name="out_proj",
    )(*ins)


MOE_BLOCK = 512
MOE_SUB = 256
MOE_TILE = 256
WAIT_GROUP = 8
MOE_K_CHUNK = 1024


def _row_pitch(d):
    return d // LANES + SUBLANES


def _store_token_rows(ref2d, val, row0, pitch):
    tm = val.shape[0]
    for cc in range(val.shape[1] // LANES):
        ref2d[pl.ds(row0 + cc, tm, stride=pitch), :] = val[:, cc * LANES:(cc + 1) * LANES]


def _router_kernel(pitch, h_ref, g_ref, w_ref, b_ref, xp_ref, lg_ref):
    xn = _rms(h_ref[...], g_ref[...])
    _store_token_rows(xp_ref, xn, 0, pitch)
    lg_ref[...] = jnp.dot(xn, w_ref[...], preferred_element_type=F32,
                          precision=lax.Precision.HIGHEST) + b_ref[...]


def _router(h, g, w_rg, b_rg):
    m, d = h.shape
    tm = _pick(m, 256, 16)
    pitch = _row_pitch(d)
    return pl.pallas_call(
        functools.partial(_router_kernel, pitch),
        grid=(m // tm,),
        in_specs=[
            pl.BlockSpec((tm, d), lambda i: (i, 0)),
            pl.BlockSpec((1, d), lambda i: (0, 0)),
            pl.BlockSpec((d, LANES), lambda i: (0, 0)),
            pl.BlockSpec((1, LANES), lambda i: (0, 0)),
        ],
        out_specs=[pl.BlockSpec((tm * pitch, LANES), lambda i: (i, 0)),
                   pl.BlockSpec((tm, LANES), lambda i: (i, 0))],
        out_shape=[SDS((m * pitch, LANES), F32), SDS((m, LANES), F32)],
        compiler_params=_cparams(("arbitrary",)),
        name="ffn_router",
    )(h, g, w_rg, b_rg)


def _dispatch_kernel(tile, pitch, n_chunks, nu_ref, nv_ref, idx_ref, xp_hbm, o_ref, stage, sem):
    i = pl.program_id(0)
    nu = nu_ref[0]
    slot_rows = tile * pitch

    def copy(t, r, slot):
        src = pl.multiple_of(idx_ref[t * tile + r] * pitch, SUBLANES)
        dst = pl.multiple_of(slot * slot_rows + r * pitch, SUBLANES)
        return pltpu.make_async_copy(
            xp_hbm.at[pl.ds(src, n_chunks)], stage.at[pl.ds(dst, n_chunks)], sem.at[slot])

    def start_tile(t, slot):
        nv = nv_ref[t]

        def start(r2, c):
            copy(t, 2 * r2, slot).start(priority=0)
            copy(t, 2 * r2 + 1, slot).start(priority=1)
            return c
        lax.fori_loop(0, nv // 2, start, 0)

        @pl.when(nv % 2 == 1)
        def _():
            copy(t, nv - 1, slot).start(priority=0)

    @pl.when(i == 0)
    def _():
        stage[...] = jnp.zeros(stage.shape, F32)
        start_tile(0, 0)

    @pl.when(i < nu)
    def _():
        slot = i % 2

        @pl.when(i + 1 < nu)
        def _():
            start_tile(i + 1, 1 - slot)

        nv = nv_ref[i]
        group = pltpu.make_async_copy(
            xp_hbm.at[pl.ds(0, WAIT_GROUP * n_chunks)],
            stage.at[pl.ds(pl.multiple_of(slot * slot_rows, SUBLANES), WAIT_GROUP * n_chunks)], sem.at[slot])

        def wait_group(g, c):
            group.wait()
            return c
        lax.fori_loop(0, nv // WAIT_GROUP, wait_group, 0)

        def wait(r, c):
            copy(i, r, slot).wait()
            return c
        lax.fori_loop(0, nv % WAIT_GROUP, wait, 0)
        for c in range(n_chunks):
            v = stage[pl.ds(slot * slot_rows + c, tile, stride=pitch), :]
            o_ref[:, c * LANES:(c + 1) * LANES] = v.astype(BF16)


def _dispatch(xp, idx, n_used_tiles, n_valid, tile, d):
    n = idx.shape[0]
    nch = d // LANES
    pitch = _row_pitch(d)

    def blk(i, nu, nv, ix):
        return (jnp.minimum(i, nu[0] - 1), 0)

    return pl.pallas_call(
        functools.partial(_dispatch_kernel, tile, pitch, nch),
        grid_spec=pltpu.PrefetchScalarGridSpec(
            num_scalar_prefetch=3,
            grid=(n // tile,),
            in_specs=[pl.BlockSpec(memory_space=pl.ANY)],
            out_specs=pl.BlockSpec((tile, d), blk),
            scratch_shapes=[pltpu.VMEM((2 * tile * pitch, LANES), F32), pltpu.SemaphoreType.DMA((2,))],
        ),
        out_shape=SDS((n, d), BF16),
        compiler_params=_cparams(("arbitrary",)),
        name="moe_dispatch",
    )(n_used_tiles, n_valid, idx, xp)


def _weight_stream(be_ref, nu, parts, sem, b, f, nf):
    t = b * nf + f

    def copies(bb, ff, slot):
        e = be_ref[bb]
        return [pltpu.make_async_copy(src(e, ff), dst(slot), sem.at[slot, n])
                for n, (src, dst) in enumerate(parts)]

    def start(bb, ff, slot):
        for n, cp in enumerate(copies(bb, ff, slot)):
            cp.start(priority=n % 2)

    @pl.when(t == 0)
    def _():
        start(0, 0, 0)

    slot = t % 2
    t_next = t + 1
    b_next = t_next // nf

    @pl.when(b_next < nu)
    def _():
        start(b_next, t_next % nf, 1 - slot)

    for cp in copies(b, f, slot):
        cp.wait()
    return slot


def _moe_up_kernel(sub, nk, be_ref, nu_ref, nvb_ref, x_ref, wg_hbm, wu_hbm, hid_ref, wbuf, acc_ref, sem):
    b = pl.program_id(0)
    kc = pl.program_id(1)
    nu = nu_ref[0]
    tk = x_ref.shape[-1]
    de = hid_ref.shape[-1]

    def row_chunk(w_hbm):
        return lambda e, kk: w_hbm.at[e, pl.ds(pl.multiple_of(kk * tk, tk), tk), :]

    @pl.when(b < nu)
    def _():
        slot = _weight_stream(
            be_ref, nu,
            [(row_chunk(wg_hbm), lambda s: wbuf.at[s, 0]), (row_chunk(wu_hbm), lambda s: wbuf.at[s, 1])],
            sem, b, kc, nk)
        w_gu = jnp.concatenate([wbuf[slot, 0].astype(BF16), wbuf[slot, 1].astype(BF16)], axis=1)

        def sub_block(s):
            rs = slice(s * sub, (s + 1) * sub)
            part = _dot(x_ref[rs, :], w_gu)

            def finish(gu):
                hid_ref[rs, :] = (jax.nn.silu(gu[:, 0:de]) * gu[:, de:2 * de]).astype(BF16)

            if nk == 1:
                finish(part)
                return

            @pl.when(kc == 0)
            def _():
                acc_ref[rs, :] = part

            @pl.when((kc > 0) & (kc < nk - 1))
            def _():
                acc_ref[rs, :] += part

            @pl.when(kc == nk - 1)
            def _():
                finish(acc_ref[rs, :] + part)

        sub_block(0)
        for s in range(1, x_ref.shape[0] // sub):
            pl.when(nvb_ref[b] > s * sub)(functools.partial(sub_block, s))


def _moe_down_kernel(sub, pitch, nj, be_ref, nu_ref, nvb_ref, hid_ref, wd_hbm, yp_ref, wbuf, sem):
    b = pl.program_id(0)
    j = pl.program_id(1)
    nu = nu_ref[0]
    _, de, tn = wbuf.shape
    half = de // 2

    def row_half(h):
        return (lambda e, jj: wd_hbm.at[e, pl.ds(h * half, half), pl.ds(pl.multiple_of(jj * tn, tn), tn)],
                lambda s: wbuf.at[s, pl.ds(h * half, half)])

    @pl.when(b < nu)
    def _():
        slot = _weight_stream(be_ref, nu, [row_half(0), row_half(1)], sem, b, j, nj)
        w_d = wbuf[slot].astype(BF16)

        def sub_block(s):
            y = _dot(hid_ref[s * sub:(s + 1) * sub, :], w_d)
            _store_token_rows(yp_ref, y, s * sub * pitch + j * (tn // LANES), pitch)

        sub_block(0)
        for s in range(1, hid_ref.shape[0] // sub):
            pl.when(nvb_ref[b] > s * sub)(functools.partial(sub_block, s))


def _moe_ffn(x_sorted, blk_expert, n_used, n_valid_blk, w_gate, w_up, w_down, tm):
    a_pad, d = x_sorted.shape
    n_exp, _, de = w_gate.shape
    n_blocks = a_pad // tm
    sub = min(MOE_SUB, tm)

    def blk(b, nu):
        return jnp.minimum(b, nu[0] - 1)

    def frozen(b, f, nu, last):
        return jnp.where(b < nu[0], f, last)

    tk = _pick(d, MOE_K_CHUNK, LANES)
    nk = d // tk
    hid = pl.pallas_call(
        functools.partial(_moe_up_kernel, sub, nk),
        grid_spec=pltpu.PrefetchScalarGridSpec(
            num_scalar_prefetch=3,
            grid=(n_blocks, nk),
            in_specs=[
                pl.BlockSpec((tm, tk), lambda b, k, be, nu, nv: (blk(b, nu), frozen(b, k, nu, nk - 1))),
                pl.BlockSpec(memory_space=pl.ANY),
                pl.BlockSpec(memory_space=pl.ANY),
            ],
            out_specs=pl.BlockSpec((tm, de), lambda b, k, be, nu, nv: (blk(b, nu), 0)),
            scratch_shapes=[pltpu.VMEM((2, 2, tk, de), F32), pltpu.VMEM((tm, 2 * de), F32),
                            pltpu.SemaphoreType.DMA((2, 2))],
        ),
        out_shape=SDS((a_pad, de), BF16),
        compiler_params=_cparams(("arbitrary", "arbitrary")),
        name="moe_up",
    )(blk_expert, n_used, n_valid_blk, x_sorted, w_gate, w_up)

    tn = _pick(d, 1024, LANES)
    nj = d // tn
    pitch = _row_pitch(d)
    return pl.pallas_call(
        functools.partial(_moe_down_kernel, sub, pitch, nj),
        grid_spec=pltpu.PrefetchScalarGridSpec(
            num_scalar_prefetch=3,
            grid=(n_blocks, nj),
            in_specs=[
                pl.BlockSpec((tm, de), lambda b, j, be, nu, nv: (blk(b, nu), 0)),
                pl.BlockSpec(memory_space=pl.ANY),
            ],
            out_specs=pl.BlockSpec((tm * pitch, LANES), lambda b, j, be, nu, nv: (blk(b, nu), 0)),
            scratch_shapes=[pltpu.VMEM((2, de, tn), F32), pltpu.SemaphoreType.DMA((2, 2))],
        ),
        out_shape=SDS((a_pad * pitch, LANES), F32),
        compiler_params=_cparams(("arbitrary", "arbitrary")),
        name="moe_down",
    )(blk_expert, n_used, n_valid_blk, hid, w_down)


def _combine_kernel(tile, pitch, n_chunks, pos_ref, h_ref, wt_ref, yp_hbm, h2_ref, h2b_ref, ybuf, sem):
    i = pl.program_id(0)
    n_tiles = pl.num_programs(0)
    n_rows = TOP_K * tile
    slot_rows = n_rows * pitch

    def copy(t, r, slot):
        src = pl.multiple_of(pos_ref[t * n_rows + r] * pitch, SUBLANES)
        dst = pl.multiple_of(slot * slot_rows + r * pitch, SUBLANES)
        return pltpu.make_async_copy(
            yp_hbm.at[pl.ds(src, n_chunks)], ybuf.at[pl.ds(dst, n_chunks)], sem.at[slot])

    def start_tile(t, slot):
        def start(r2, c):
            copy(t, 2 * r2, slot).start(priority=0)
            copy(t, 2 * r2 + 1, slot).start(priority=1)
            return c
        lax.fori_loop(0, n_rows // 2, start, 0, unroll=4)

    @pl.when(i == 0)
    def _():
        start_tile(0, 0)

    slot = i % 2

    @pl.when(i + 1 < n_tiles)
    def _():
        start_tile(i + 1, 1 - slot)

    group = pltpu.make_async_copy(
        yp_hbm.at[pl.ds(0, WAIT_GROUP * n_chunks)],
        ybuf.at[pl.ds(pl.multiple_of(slot * slot_rows, SUBLANES), WAIT_GROUP * n_chunks)], sem.at[slot])
    for _ in range(n_rows // WAIT_GROUP):
        group.wait()
    for r in range(n_rows % WAIT_GROUP):
        copy(i, n_rows - 1 - r, slot).wait()
    wt = wt_ref[...]
    for c in range(n_chunks):
        cs = slice(c * LANES, (c + 1) * LANES)
        h2 = h_ref[:, cs]
        for kk in range(TOP_K):
            row0 = slot * slot_rows + kk * tile * pitch + c
            h2 = h2 + wt[:, kk:kk + 1] * ybuf[pl.ds(row0, tile, stride=pitch), :]
        h2_ref[:, cs] = h2
        h2b_ref[:, cs] = h2.astype(BF16)


def _combine(h, yp, pos, wts):
    m, d = h.shape
    tile = _pick(m, 256, 16)
    n_tiles = m // tile
    pitch = _row_pitch(d)
    pos_tiled = pos.reshape(n_tiles, tile, TOP_K).transpose(0, 2, 1).reshape(-1)
    return pl.pallas_call(
        functools.partial(_combine_kernel, tile, pitch, d // LANES),
        grid_spec=pltpu.PrefetchScalarGridSpec(
            num_scalar_prefetch=1,
            grid=(n_tiles,),
            in_specs=[
                pl.BlockSpec((tile, d), lambda i, p: (i, 0)),
                pl.BlockSpec((tile, TOP_K), lambda i, p: (i, 0)),
                pl.BlockSpec(memory_space=pl.ANY),
            ],
            out_specs=[pl.BlockSpec((tile, d), lambda i, p: (i, 0)),
                       pl.BlockSpec((tile, d), lambda i, p: (i, 0))],
            scratch_shapes=[pltpu.VMEM((2 * TOP_K * tile * pitch, LANES), F32),
                            pltpu.SemaphoreType.DMA((2,))],
        ),
        out_shape=[SDS((m, d), F32), SDS((m, d), BF16)],
        compiler_params=_cparams(("arbitrary",)),
        name="moe_combine",
    )(pos_tiled, h, wts, yp)


def _ple_kernel(final_norm, nj, hb_ref, h_ref, p_ref, wg_ref, wp_ref, g_ref, o_ref, acc_ref):
    j = pl.program_id(1)
    gate = jax.nn.sigmoid(_dot(hb_ref[...], wg_ref[...]))
    proj = _dot(p_ref[...].astype(BF16), wp_ref[...])
    acc_ref[j] = h_ref[...] + gate * proj

    @pl.when(j == nj - 1)
    def _():
        tn = acc_ref.shape[-1]
        if final_norm:
            ssq = jnp.zeros((acc_ref.shape[1], 1), F32)
            for c in range(nj):
                a = acc_ref[c]
                ssq = ssq + jnp.sum(a * a, axis=-1, keepdims=True)
            rs = lax.rsqrt(ssq / (nj * tn) + EPS)
            for c in range(nj):
                o_ref[:, c * tn:(c + 1) * tn] = acc_ref[c] * rs * g_ref[:, c * tn:(c + 1) * tn]
        else:
            for c in range(nj):
                o_ref[:, c * tn:(c + 1) * tn] = acc_ref[c]


def _ple(h2b, h2, p, w_gate, w_proj, g, rows0, rows, final_norm):
    d = h2.shape[1]
    pd = p.shape[1]
    tm = _pick(math.gcd(rows, rows0) if rows0 else rows, 512, 16)
    tn = _pick(d, 512, LANES)
    nj = d // tn
    ro = rows0 // tm
    return pl.pallas_call(
        functools.partial(_ple_kernel, final_norm, nj),
        grid=(rows // tm, nj),
        in_specs=[
            pl.BlockSpec((tm, d), lambda i, j: (i + ro, 0)),
            pl.BlockSpec((tm, tn), lambda i, j: (i + ro, j)),
            pl.BlockSpec((tm, pd), lambda i, j: (i, 0)),
            pl.BlockSpec((d, tn), lambda i, j: (0, j)),
            pl.BlockSpec((pd, tn), lambda i, j: (0, j)),
            pl.BlockSpec((1, d), lambda i, j: (0, 0)),
        ],
        out_specs=pl.BlockSpec((tm, d), lambda i, j: (i, 0)),
        out_shape=SDS((rows, d), F32),
        scratch_shapes=[pltpu.VMEM((nj, tm, tn), F32)],
        compiler_params=_cparams(("arbitrary", "arbitrary")),
        name="ple_gate",
    )(h2b, h2, p, w_gate, w_proj, g)


def _route(logits, n_groups, epg, tm, tile):
    m = logits.shape[0]
    n_exp = n_groups * epg
    g_logits = logits[:, :n_groups]
    g_idx = jnp.argmax(g_logits, axis=-1)
    g_w = 1.0 / jnp.sum(jnp.exp(g_logits - jnp.max(g_logits, axis=-1, keepdims=True)), axis=-1)
    e_logits = logits[:, n_groups:n_groups + n_exp].reshape(m, n_groups, epg)
    in_group = (jnp.arange(n_groups)[None, :] == g_idx[:, None])[:, :, None]
    e_in = jnp.sum(jnp.where(in_group, e_logits, 0.0), axis=1)
    lane = jnp.arange(epg)[None, :]
    first = jnp.argmax(e_in, axis=-1)
    v1 = jnp.max(e_in, axis=-1)
    rest = jnp.where(lane == first[:, None], NEG_INF, e_in)
    second = jnp.argmax(rest, axis=-1)
    v2 = jnp.max(rest, axis=-1)
    w2 = jnp.exp(v2 - v1)
    e_w = jnp.stack([1.0 / (1.0 + w2), w2 / (1.0 + w2)], axis=-1)
    ids = g_idx[:, None] * epg + jnp.stack([first, second], axis=-1)
    wts = g_w[:, None] * e_w

    a = m * TOP_K
    e_flat = ids.reshape(-1).astype(jnp.int32)
    onehot = (e_flat[:, None] == jnp.arange(n_exp, dtype=jnp.int32)[None, :]).astype(jnp.int32)
    counts = jnp.sum(onehot, axis=0)
    rank = jnp.sum((jnp.cumsum(onehot, axis=0) - onehot) * onehot, axis=1)
    padded = (counts + tm - 1) // tm * tm
    pad_end = jnp.cumsum(padded)
    pad_start = pad_end - padded
    dest = (pad_start[e_flat] + rank).astype(jnp.int32)
    n_blocks = -(-a // tm) + n_exp
    tok = jnp.arange(a, dtype=jnp.int32) // TOP_K
    row_src = jnp.zeros((n_blocks * tm,), jnp.int32).at[dest].set(tok)
    n_used = (pad_end[-1] // tm).astype(jnp.int32)
    blk_id = jnp.arange(n_blocks, dtype=jnp.int32)
    blk_expert = jnp.minimum(jnp.searchsorted(pad_end, blk_id * tm, side="right"), n_exp - 1)
    last = blk_expert[jnp.maximum(n_used - 1, 0)]
    blk_expert = jnp.where(blk_id < n_used, blk_expert, last).astype(jnp.int32)
    tile_id = jnp.arange(n_blocks * tm // tile, dtype=jnp.int32)
    tile_e = blk_expert[tile_id * tile // tm]
    valid_end = (pad_start + counts)[tile_e]
    n_valid = jnp.clip(valid_end - tile_id * tile, 0, tile)
    n_valid = jnp.where(tile_id * tile < pad_end[-1], n_valid, 0).astype(jnp.int32)
    n_used_tiles = (n_used * (tm // tile)).reshape(1)
    n_valid_blk = jnp.sum(n_valid.reshape(n_blocks, tm // tile), axis=1).astype(jnp.int32)
    return (row_src, dest.reshape(m, TOP_K), wts, blk_expert, n_used.reshape(1), n_used_tiles, n_valid,
            n_valid_blk)


def kernel(x_prompt, x_sample, p_prompt, p_sample, cache_ckv, cache_krope, state_C, state_n, state_m, page_table, norm_mix, w_in, b_igate, b_fgate, mlstm_norm, q_a_norm, w_q_b, kv_a_norm, w_kv_b, w_out, norm_ffn, w_group, b_group, w_router, b_router, w_gate, w_up, w_down, w_ple_proj, w_ple_gate, norm_final):
    depth = w_in.shape[0]
    bp, seq, d = x_prompt.shape
    n_dec, ts, _ = x_sample.shape
    mp, ms = bp * seq, n_dec * ts
    m = mp + ms
    hm_heads = b_igate.shape[-1]
    dk, dv = state_n.shape[-1], state_C.shape[-1]
    ql, kvl, rope = q_a_norm.shape[-1], kv_a_norm.shape[-1], cache_krope.shape[-1]
    att_w = d - hm_heads * dv
    a_heads = (w_q_b.shape[-1] - w_kv_b.shape[-1] + att_w) // rope
    av = att_w // a_heads
    n_groups, n_exp = w_group.shape[-1], w_router.shape[-1]
    epg = n_exp // n_groups
    n_pages, page = page_table.shape[1], cache_ckv.shape[2]
    past = n_pages * page
    wq, wv = hm_heads * dk, hm_heads * dv
    assert rope == LANES // 2 and a_heads % 2 == 0 and w_q_b.shape[-1] == a_heads * (A_NOPE + rope)
    assert dv == 2 * dk and mp % ms == 0 and n_groups + n_exp <= LANES and 2 * hm_heads <= LANES // 2

    half = rope // 2
    inv = 1.0 / (ROPE_THETA ** (jnp.arange(half, dtype=F32) / half))
    pos = jnp.concatenate([jnp.tile(jnp.arange(seq), bp), jnp.tile(past + jnp.arange(ts), n_dec)])
    ang = pos.astype(F32)[:, None] * inv[None, :]
    cos_t = jnp.tile(jnp.cos(ang), (1, LANES // half))
    sin_t = jnp.tile(jnp.concatenate([-jnp.sin(ang), jnp.sin(ang)], axis=1), (1, LANES // rope))

    hp = x_prompt.reshape(mp, d)
    hs = x_sample.reshape(ms, d)
    outs = {k: [] for k in ("ckv_p", "kr_p", "C_p", "n_p", "m_p", "ckv_s", "kr_s", "C_s", "n_s", "m_s")}
    for l in range(depth):
        wl = w_in[l]
        o_aq = 2 * wq + 2 * wv + 2 * hm_heads
        o_c = o_aq + ql
        o_kr = o_c + kvl
        w_main = jnp.concatenate([wl[:, :2 * wq + 2 * wv], wl[:, o_aq:o_kr]], axis=1).astype(BF16)
        n_small = rope + 2 * hm_heads
        w_small = jnp.concatenate(
            [wl[:, o_kr:o_kr + rope], wl[:, 2 * wq + 2 * wv:o_aq], jnp.zeros((d, LANES - n_small), F32)],
            axis=1).astype(BF16)
        gate_bias = jnp.concatenate(
            [jnp.zeros((rope,), F32), b_igate[l], b_fgate[l], jnp.zeros((LANES - n_small,), F32)]).reshape(1, LANES)
        wqb = w_q_b[l].reshape(ql, a_heads, A_NOPE + rope)
        wq_re = jnp.concatenate(
            [wqb[:, :, :A_NOPE].reshape(ql, -1), wqb[:, :, A_NOPE:].reshape(ql, -1)], axis=1).astype(BF16)
        wkvb = w_kv_b[l].astype(BF16)
        wkv3 = w_kv_b[l].reshape(kvl, a_heads, A_NOPE + av)
        w_ukt = wkv3[:, :, :A_NOPE].transpose(1, 2, 0).astype(BF16)
        w_uv = wkv3[:, :, A_NOPE:].transpose(1, 0, 2).astype(BF16)
        w_o = w_out[l].astype(BF16)
        w_rg = jnp.concatenate(
            [w_group[l], w_router[l], jnp.zeros((d, LANES - n_groups - n_exp), F32)], axis=1)
        b_rg = jnp.concatenate(
            [b_group[l], b_router[l], jnp.zeros((LANES - n_groups - n_exp,), F32)]).reshape(1, LANES)
        w_pg = w_ple_gate[l].astype(BF16)
        w_pp = w_ple_proj[l].astype(BF16)
        g_mix = norm_mix[l].reshape(1, d)

        zz = _in_proj(hp, g_mix, w_main, w_small, m, 0, None)
        z, zs = _in_proj(hs, g_mix, w_main, w_small, m, mp, zz)
        q_hm, ckv, kr = _mla_q(z, zs, cos_t, sin_t, q_a_norm[l].reshape(1, ql), kv_a_norm[l].reshape(1, kvl),
                               wq_re, a_heads, ql, kvl, rope, 2 * wq + 2 * wv, 2 * wq + 2 * wv + ql)
        ckv_p, ckv_s, kr_p, kr_s = ckv[:mp], ckv[mp:], kr[:mp], kr[mp:]

        norm_m = mlstm_norm[l].reshape(1, wv)
        lp = _pick(seq, 256, 16)
        hm, c_p, n_p, m_p = _mlstm(z, zs, gate_bias, norm_m, bp, seq, lp, 1, 0, hm_heads, dk, dv, m, None, None)
        n_sub = 2 if n_dec % 2 == 0 else 1
        hm, c_s, n_s, m_s = _mlstm(z, zs, gate_bias, norm_m, n_dec, ts, ts, n_sub, mp, hm_heads, dk, dv, m,
                                   (state_C[l], state_n[l], state_m[l].reshape(n_dec, 1, hm_heads)), hm)

        k_hm, v_hm = _kv_expand(ckv_p, kr_p, wkvb, mp, a_heads, av)
        att = _flash_prompt(q_hm, k_hm, v_hm, bp, seq, m)
        qs = _q_latent(q_hm, w_ukt, mp, n_dec, ts)
        o_s = _mla_sample(page_table, qs, ckv_s, kr_s, cache_ckv[l], jnp.swapaxes(cache_krope[l], 1, 2))
        att = _o_uv(o_s, w_uv, att, mp)

        h1 = _out_proj(hm, att, w_o, hp, 0, m, None)
        h1 = _out_proj(hm, att, w_o, hs, mp, m, h1)

        xn2p, logits = _router(h1, norm_ffn[l].reshape(1, d), w_rg, b_rg)
        row_src, posn, wts, blk_expert, n_used, n_used_tiles, n_valid, n_valid_blk = _route(
            logits, n_groups, epg, MOE_BLOCK, MOE_TILE)
        x_sorted = _dispatch(xn2p, row_src, n_used_tiles, n_valid, MOE_TILE, d)
        yp = _moe_ffn(x_sorted, blk_expert, n_used, n_valid_blk, w_gate[l], w_up[l], w_down[l], MOE_BLOCK)
        h2, h2b = _combine(h1, yp, posn, wts)

        last = l == depth - 1
        g_fin = norm_final.reshape(1, d)
        hp = _ple(h2b, h2, p_prompt[l].reshape(mp, -1), w_pg, w_pp, g_fin, 0, mp, last)
        hs = _ple(h2b, h2, p_sample[l].reshape(ms, -1), w_pg, w_pp, g_fin, mp, ms, last)

        outs["ckv_p"].append(ckv_p.reshape(bp, seq, kvl))
        outs["kr_p"].append(kr_p.reshape(bp, seq, rope))
        outs["C_p"].append(c_p)
        outs["n_p"].append(n_p)
        outs["m_p"].append(m_p.reshape(bp, hm_heads))
        outs["ckv_s"].append(ckv_s.reshape(n_dec, ts, kvl))
        outs["kr_s"].append(kr_s.reshape(n_dec, ts, rope))
        outs["C_s"].append(c_s)
        outs["n_s"].append(n_s)
        outs["m_s"].append(m_s.reshape(n_dec, hm_heads))

    st = {k: jnp.stack(v) for k, v in outs.items()}
    return (hp.reshape(bp, seq, d), hs.reshape(n_dec, ts, d),
            st["ckv_p"], st["kr_p"], st["C_p"], st["n_p"], st["m_p"],
            st["ckv_s"], st["kr_s"], st["C_s"], st["n_s"], st["m_s"])
```

```python
import functools
import math

import jax
import jax.numpy as jnp
from jax import lax
from jax.experimental import pallas as pl
from jax.experimental.pallas import tpu as pltpu

F32 = jnp.float32
BF16 = jnp.bfloat16
SDS = jax.ShapeDtypeStruct

EPS = 1e-6
GATE_CAP = 15.0
ROPE_THETA = 10000.0
A_NOPE = 128
TOP_K = 2

LANES = 128
SUBLANES = 8
VMEM_LIMIT = 56 * 1024 * 1024
NEG_INF = float("-inf")

NT_DIMS = (((1,), (1,)), ((), ()))


def _cparams(sem, vmem=VMEM_LIMIT):
    return pltpu.CompilerParams(dimension_semantics=sem, vmem_limit_bytes=vmem)


def _pick(n, pref, mult):
    t = min(pref, n) // mult * mult
    while t >= mult:
        if n % t == 0:
            return t
        t -= mult
    return n


def _rup(x, m):
    return (x + m - 1) // m * m


def _dot(a, b):
    return jnp.dot(a, b, preferred_element_type=F32)


def _dot_nt(a, b, precision=None):
    return lax.dot_general(a, b, NT_DIMS, preferred_element_type=F32, precision=precision)


def _rms(x, g):
    ms = jnp.mean(x * x, axis=-1, keepdims=True)
    return x * lax.rsqrt(ms + EPS) * g


def _inproj_kernel(x_ref, g_ref, w_ref, ws_ref, *rest):
    z_ref, zs_ref, xn_ref = rest[-3:]
    j = pl.program_id(1)

    @pl.when(j == 0)
    def _():
        xn = _rms(x_ref[...], g_ref[...]).astype(BF16)
        xn_ref[...] = xn
        zs_ref[...] = _dot(xn, ws_ref[...])

    z_ref[...] = _dot(xn_ref[...], w_ref[...])


def _in_proj(x, g, w_main, w_small, m_total, row0, prev):
    rows, d = x.shape
    nm = w_main.shape[1]
    tm = _pick(math.gcd(rows, row0) if row0 else rows, 512, 16)
    tn = _pick(nm, 512, LANES)
    ro = row0 // tm
    ins = [x, g, w_main, w_small]
    in_specs = [
        pl.BlockSpec((tm, d), lambda i, j: (i, 0)),
        pl.BlockSpec((1, d), lambda i, j: (0, 0)),
        pl.BlockSpec((d, tn), lambda i, j: (0, j)),
        pl.BlockSpec((d, LANES), lambda i, j: (0, 0)),
    ]
    aliases = {}
    if prev is not None:
        ins += list(prev)
        in_specs += [pl.BlockSpec(memory_space=pl.ANY)] * 2
        aliases = {4: 0, 5: 1}
    return pl.pallas_call(
        _inproj_kernel,
        grid=(rows // tm, nm // tn),
        in_specs=in_specs,
        out_specs=[pl.BlockSpec((tm, tn), lambda i, j: (i + ro, j)),
                   pl.BlockSpec((tm, LANES), lambda i, j: (i + ro, 0))],
        out_shape=[SDS((m_total, nm), F32), SDS((m_total, LANES), F32)],
        scratch_shapes=[pltpu.VMEM((tm, d), BF16)],
        input_output_aliases=aliases,
        compiler_params=_cparams(("arbitrary", "arbitrary")),
        name="in_proj",
    )(*ins)


def _rope_lanes(y, cos, sin_signed, width):
    half = 32
    lane = lax.broadcasted_iota(jnp.int32, y.shape, 1)
    first = (lane % (2 * half)) < half
    swapped = jnp.where(first, pltpu.roll(y, width - half, 1), pltpu.roll(y, half, 1))
    return y * cos + swapped * sin_signed


def _mlaq_kernel(n_heads, rope, aq_ref, c_ref, zs_ref, cos_ref, sin_ref, qg_ref, kvg_ref, wq_ref,
                 q_ref, ckv_ref, kr_ref):
    aqn = _rms(aq_ref[...], qg_ref[...]).astype(BF16)
    q_scale = (A_NOPE + rope) ** -0.5 * math.log2(math.e)
    y = _dot(aqn, wq_ref[...]) * q_scale
    cos = cos_ref[...]
    sin = sin_ref[...]
    nope_w = n_heads * A_NOPE
    for h2 in range(n_heads // 2):
        yr = y[:, nope_w + h2 * LANES: nope_w + (h2 + 1) * LANES]
        rot = _rope_lanes(yr, cos, sin, LANES)
        for s in range(2):
            h = 2 * h2 + s
            q_ref[h, :, 0:A_NOPE] = y[:, h * A_NOPE:(h + 1) * A_NOPE].astype(BF16)
            q_ref[h, :, A_NOPE:A_NOPE + rope] = rot[:, s * rope:(s + 1) * rope].astype(BF16)
    ckv_ref[...] = _rms(c_ref[...], kvg_ref[...])
    kr = _rope_lanes(zs_ref[...], cos, sin, LANES)
    kr_ref[...] = kr[:, 0:rope]


def _mla_q(z, zs, cos, sin, qg, kvg, wq, n_heads, ql, kvl, rope, aq_off, c_off):
    m = z.shape[0]
    tm = _pick(m, 256, 16)
    qd = A_NOPE + rope
    return pl.pallas_call(
        functools.partial(_mlaq_kernel, n_heads, rope),
        grid=(m // tm,),
        in_specs=[
            pl.BlockSpec((tm, ql), lambda i: (i, aq_off // ql)),
            pl.BlockSpec((tm, kvl), lambda i: (i, c_off // kvl)),
            pl.BlockSpec((tm, LANES), lambda i: (i, 0)),
            pl.BlockSpec((tm, LANES), lambda i: (i, 0)),
            pl.BlockSpec((tm, LANES), lambda i: (i, 0)),
            pl.BlockSpec((1, ql), lambda i: (0, 0)),
            pl.BlockSpec((1, kvl), lambda i: (0, 0)),
            pl.BlockSpec((ql, n_heads * qd), lambda i: (0, 0)),
        ],
        out_specs=[
            pl.BlockSpec((n_heads, tm, qd), lambda i: (0, i, 0)),
            pl.BlockSpec((tm, kvl), lambda i: (i, 0)),
            pl.BlockSpec((tm, rope), lambda i: (i, 0)),
        ],
        out_shape=[SDS((n_heads, m, qd), BF16), SDS((m, kvl), F32), SDS((m, rope), F32)],
        compiler_params=_cparams(("arbitrary",)),
        name="mla_q",
    )(z, z, zs, cos, sin, qg, kvg, wq)


def _kv_kernel(n_heads, av, rope, c_ref, kr_ref, w_ref, k_ref, v_ref):
    y = _dot(c_ref[...].astype(BF16), w_ref[...])
    kr = kr_ref[...].astype(BF16)
    hw = A_NOPE + av
    for h in range(n_heads):
        k_ref[h, :, 0:A_NOPE] = y[:, h * hw: h * hw + A_NOPE].astype(BF16)
        k_ref[h, :, A_NOPE:A_NOPE + rope] = kr
        v_ref[h] = y[:, h * hw + A_NOPE:(h + 1) * hw].astype(BF16)


def _kv_expand(ckv, kr, w_kvb, rows, n_heads, av):
    kvl = ckv.shape[1]
    rope = kr.shape[1]
    tm = _pick(rows, 256, 16)
    return pl.pallas_call(
        functools.partial(_kv_kernel, n_heads, av, rope),
        grid=(rows // tm,),
        in_specs=[
            pl.BlockSpec((tm, kvl), lambda i: (i, 0)),
            pl.BlockSpec((tm, rope), lambda i: (i, 0)),
            pl.BlockSpec(w_kvb.shape, lambda i: (0, 0)),
        ],
        out_specs=[
            pl.BlockSpec((n_heads, tm, A_NOPE + rope), lambda i: (0, i, 0)),
            pl.BlockSpec((n_heads, tm, av), lambda i: (0, i, 0)),
        ],
        out_shape=[SDS((n_heads, rows, A_NOPE + rope), BF16), SDS((n_heads, rows, av), BF16)],
        compiler_params=_cparams(("arbitrary",)),
        name="kv_expand",
    )(ckv, kr, w_kvb)


def _flash_kernel(tq, tk, q_ref, k_ref, v_ref, o_ref):
    qi = pl.program_id(2)
    q = q_ref[0]
    av = v_ref.shape[-1]
    row = qi * tq + lax.broadcasted_iota(jnp.int32, (tq, tk), 0)
    col0 = lax.broadcasted_iota(jnp.int32, (tq, tk), 1)

    def body(masked, kb, carry):
        m, l, acc = carry
        start = pl.multiple_of(kb * tk, tk)
        k = k_ref[0, pl.ds(start, tk), :]
        v = v_ref[0, pl.ds(start, tk), :]
        s = _dot_nt(q, k)
        if masked:
            s = jnp.where(col0 + kb * tk <= row, s, NEG_INF)
        m_new = jnp.maximum(m, jnp.max(s, axis=-1, keepdims=True))
        p = jnp.exp2(s - m_new)
        alpha = jnp.exp2(m - m_new)
        l = alpha * l + jnp.sum(p, axis=-1, keepdims=True)
        acc = alpha * acc + _dot(p.astype(BF16), v)
        return m_new, l, acc

    n_full = (qi * tq) // tk
    n_kb = (qi * tq + tq + tk - 1) // tk
    init = (jnp.full((tq, 1), NEG_INF, F32), jnp.zeros((tq, 1), F32), jnp.zeros((tq, av), F32))
    carry = lax.fori_loop(0, n_full, functools.partial(body, False), init)
    _, l, acc = lax.fori_loop(n_full, n_kb, functools.partial(body, True), carry)
    o_ref[...] = (acc / l).astype(o_ref.dtype)


def _flash_prompt(q_hm, k_hm, v_hm, n_batch, seq, m_total):
    n_heads, _, qd = q_hm.shape
    av = v_hm.shape[-1]
    tq = _pick(seq, 512, 16)
    tk = _pick(seq, 512, 16)
    nq = seq // tq
    return pl.pallas_call(
        functools.partial(_flash_kernel, tq, tk),
        grid=(n_batch, n_heads, nq),
        in_specs=[
            pl.BlockSpec((1, tq, qd), lambda b, h, i: (h, b * nq + i, 0)),
            pl.BlockSpec((1, seq, qd), lambda b, h, i: (h, b, 0)),
            pl.BlockSpec((1, seq, av), lambda b, h, i: (h, b, 0)),
        ],
        out_specs=pl.BlockSpec((tq, av), lambda b, h, i: (b * nq + i, h)),
        out_shape=SDS((m_total, n_heads * av), BF16),
        compiler_params=_cparams(("arbitrary", "arbitrary", "arbitrary")),
        name="flash_prompt",
    )(q_hm, k_hm, v_hm)


def _qlat_kernel(n_seq, ts, kvl, rope, q_ref, w_ref, o_ref):
    q = q_ref[0]
    qlat = _dot(q[:, 0:A_NOPE], w_ref[0])
    width = o_ref.shape[-1]
    full = jnp.concatenate(
        [qlat, q[:, A_NOPE:A_NOPE + rope].astype(F32),
         jnp.zeros((q.shape[0], width - kvl - rope), F32)], axis=1)
    o_ref[...] = full.reshape(n_seq, 1, ts, width)


def _q_latent(q_hm, w_ukt, rows0, n_seq, ts):
    n_heads, _, qd = q_hm.shape
    kvl = w_ukt.shape[-1]
    rope = qd - A_NOPE
    ms = n_seq * ts
    width = _rup(kvl + rope, LANES)
    return pl.pallas_call(
        functools.partial(_qlat_kernel, n_seq, ts, kvl, rope),
        grid=(n_heads,),
        in_specs=[
            pl.BlockSpec((1, ms, qd), lambda h: (h, rows0 // ms, 0)),
            pl.BlockSpec((1, A_NOPE, kvl), lambda h: (h, 0, 0)),
        ],
        out_specs=pl.BlockSpec((n_seq, 1, ts, width), lambda h: (0, h, 0, 0)),
        out_shape=SDS((n_seq, n_heads, ts, width), F32),
        compiler_params=_cparams(("arbitrary",)),
        name="q_latent",
    )(q_hm, w_ukt)


MLA_BUFFERS = 4
MLA_STREAMS = 4


def _mla_sample_kernel(n_pages, chunk_pages, page, n_seq, ts, kvl, rope,
                       pt_ref, q_ref, cn_ref, kn_ref, ckv_hbm, krt_hbm, o_ref,
                       kbuf, rbuf, sem, *stats):
    m_ref, l_ref, acc_ref = stats[0::3], stats[1::3], stats[2::3]
    b = pl.program_id(0)
    n_chunks = n_pages // chunk_pages
    n_total = n_seq * n_chunks
    rows = q_ref.shape[1] * ts
    keys = chunk_pages * page
    part = keys // MLA_STREAMS

    def chunk_copies(g, slot):
        gw = g % n_total
        base = (gw // n_chunks) * n_pages + (gw % n_chunks) * chunk_pages
        copies = []
        for p in range(chunk_pages):
            pg = pt_ref[base + p]
            copies.append(pltpu.make_async_copy(
                ckv_hbm.at[pg], kbuf.at[slot, pl.ds(p * page, page)], sem.at[slot]))
            copies.append(pltpu.make_async_copy(
                krt_hbm.at[pg], rbuf.at[slot, :, pl.ds(p * page, page)], sem.at[slot]))
        return copies

    def start_chunk(g, slot):
        for n, cp in enumerate(chunk_copies(g, slot)):
            cp.start(priority=(n // 2 + n) % 2)

    @pl.when(b == 0)
    def _():
        for g0 in range(MLA_BUFFERS - 1):
            start_chunk(g0, g0)

    q = q_ref[0].reshape(rows, q_ref.shape[-1])
    ql = q[:, 0:kvl].astype(BF16)
    qr = q[:, kvl:kvl + rope].astype(BF16)
    for j in range(MLA_STREAMS):
        m_ref[j][...] = jnp.full(m_ref[j].shape, NEG_INF, F32)
        l_ref[j][...] = jnp.zeros(l_ref[j].shape, F32)
        acc_ref[j][...] = jnp.zeros(acc_ref[j].shape, F32)

    def softmax_step(j, s):
        m = m_ref[j][...]
        m_new = jnp.maximum(m, jnp.max(s, axis=-1, keepdims=True))
        p = jnp.exp2(s - m_new)
        alpha = jnp.exp2(m - m_new)
        l_ref[j][...] = alpha * l_ref[j][...] + jnp.sum(p, axis=-1, keepdims=True)
        m_ref[j][...] = m_new
        return p.astype(BF16), alpha

    def accumulate(j, p, alpha, values):
        acc_ref[j][...] = alpha * acc_ref[j][...] + _dot(p, values)

    def body(c, carry):
        g = b * n_chunks + c
        slot = g % MLA_BUFFERS
        ahead = g + MLA_BUFFERS - 1
        start_chunk(ahead, ahead % MLA_BUFFERS)
        for cp in chunk_copies(g, slot):
            cp.wait()
        ks = [kbuf[slot, j * part:(j + 1) * part, :].astype(BF16) for j in range(MLA_STREAMS)]
        ss = [_dot_nt(ql, ks[j]) + _dot(qr, rbuf[slot, :, j * part:(j + 1) * part].astype(BF16))
              for j in range(MLA_STREAMS)]
        pa = [softmax_step(j, ss[j]) for j in range(MLA_STREAMS)]
        for j in range(MLA_STREAMS):
            accumulate(j, pa[j][0], pa[j][1], ks[j])
        return carry

    lax.fori_loop(0, n_chunks, body, 0)

    @pl.when(b == n_seq - 1)
    def _():
        for d in range(MLA_BUFFERS - 1):
            for cp in chunk_copies(n_total + d, (n_total + d) % MLA_BUFFERS):
                cp.wait()

    pad = 2 * SUBLANES - ts
    cn = jnp.concatenate([cn_ref[...], jnp.zeros((pad, kvl), F32)], axis=0).astype(BF16)
    kn = jnp.concatenate([kn_ref[...], jnp.zeros((pad, rope), F32)], axis=0).astype(BF16)
    s = _dot_nt(ql, cn) + _dot_nt(qr, kn)
    tok = lax.broadcasted_iota(jnp.int32, s.shape, 0) % ts
    col = lax.broadcasted_iota(jnp.int32, s.shape, 1)
    s = jnp.where(col <= tok, s, NEG_INF)
    p_new, alpha_new = softmax_step(0, s)
    accumulate(0, p_new, alpha_new, cn)

    m_all = m_ref[0][...]
    for j in range(1, MLA_STREAMS):
        m_all = jnp.maximum(m_all, m_ref[j][...])
    l_all = jnp.zeros(m_all.shape, F32)
    acc = jnp.zeros(acc_ref[0].shape, F32)
    for j in range(MLA_STREAMS):
        w = jnp.exp2(m_ref[j][...] - m_all)
        l_all = l_all + w * l_ref[j][...]
        acc = acc + w * acc_ref[j][...]
    o_ref[0] = (acc / l_all).reshape(o_ref.shape[1:])


def _mla_sample(page_table, qs, ckv_s, kr_s, cache_ckv, cache_krt):
    n_seq, n_heads, ts, width = qs.shape
    n_pages = page_table.shape[1]
    page, kvl = cache_ckv.shape[1:]
    rope = cache_krt.shape[1]
    chunk_pages = _pick(n_pages, 32, 1)
    rows = n_heads * ts
    keys = chunk_pages * page
    assert n_seq * (n_pages // chunk_pages) >= MLA_BUFFERS and keys % (MLA_STREAMS * LANES) == 0
    kern = functools.partial(_mla_sample_kernel, n_pages, chunk_pages, page, n_seq, ts, kvl, rope)
    return pl.pallas_call(
        kern,
        grid_spec=pltpu.PrefetchScalarGridSpec(
            num_scalar_prefetch=1,
            grid=(n_seq,),
            in_specs=[
                pl.BlockSpec((1, n_heads, ts, width), lambda b, pt: (b, 0, 0, 0)),
                pl.BlockSpec((ts, kvl), lambda b, pt: (b, 0)),
                pl.BlockSpec((ts, rope), lambda b, pt: (b, 0)),
                pl.BlockSpec(memory_space=pl.ANY),
                pl.BlockSpec(memory_space=pl.ANY),
            ],
            out_specs=pl.BlockSpec((1, n_heads, ts, kvl), lambda b, pt: (b, 0, 0, 0)),
            scratch_shapes=[
                pltpu.VMEM((MLA_BUFFERS, keys, kvl), F32),
                pltpu.VMEM((MLA_BUFFERS, rope, keys), F32),
                pltpu.SemaphoreType.DMA((MLA_BUFFERS,)),
            ] + MLA_STREAMS * [pltpu.VMEM((rows, 1), F32), pltpu.VMEM((rows, 1), F32),
                               pltpu.VMEM((rows, kvl), F32)],
        ),
        out_shape=SDS((n_seq, n_heads, ts, kvl), F32),
        compiler_params=_cparams(("arbitrary",)),
        name="mla_sample",
    )(page_table.reshape(-1), qs, ckv_s, kr_s, cache_ckv, cache_krt)


def _ouv_kernel(o_ref, w_ref, prev_ref, att_ref):
    del prev_ref
    o = o_ref[...]
    o = o.reshape(o.shape[0] * o.shape[2], o.shape[3]).astype(BF16)
    att_ref[...] = _dot(o, w_ref[0]).astype(att_ref.dtype)


def _o_uv(o_s, w_uv, att, rows0):
    n_seq, n_heads, ts, kvl = o_s.shape
    av = w_uv.shape[-1]
    ms = n_seq * ts
    return pl.pallas_call(
        _ouv_kernel,
        grid=(n_heads,),
        in_specs=[
            pl.BlockSpec((n_seq, 1, ts, kvl), lambda h: (0, h, 0, 0)),
            pl.BlockSpec((1, kvl, av), lambda h: (h, 0, 0)),
            pl.BlockSpec(memory_space=pl.ANY),
        ],
        out_specs=pl.BlockSpec((ms, av), lambda h: (rows0 // ms, h)),
        out_shape=SDS(att.shape, att.dtype),
        input_output_aliases={2: 0},
        compiler_params=_cparams(("arbitrary",)),
        name="o_uv",
    )(o_s, w_uv, att)


def _mlstm_kernel(n_sub, L, n_heads, dk, dv, has_state, *refs):
    if has_state:
        (q_ref, k_ref, v_ref, mo_ref, zs_ref, bias_ref, norm_ref, c0_ref, n0_ref, m0_ref, _prev,
         hm_ref, c_ref, n_ref, m_ref) = refs
    else:
        (q_ref, k_ref, v_ref, mo_ref, zs_ref, bias_ref, norm_ref,
         hm_ref, c_ref, n_ref, m_ref) = refs
    ci = pl.program_id(1)
    Lk = _rup(L, LANES)

    @pl.when(ci == 0)
    def _():
        if has_state:
            c_ref[...] = c0_ref[...]
            n_ref[...] = n0_ref[...]
            m_ref[...] = m0_ref[...]
        else:
            c_ref[...] = jnp.zeros(c_ref.shape, F32)
            n_ref[...] = jnp.zeros(n_ref.shape, F32)
            m_ref[...] = jnp.zeros(m_ref.shape, F32)

    def pad_rows(a):
        if Lk == L:
            return a
        return jnp.concatenate([a, jnp.zeros((Lk - L, a.shape[1]), a.dtype)], axis=0)

    row = lax.broadcasted_iota(jnp.int32, (L, Lk), 0)
    col = lax.broadcasted_iota(jnp.int32, (L, Lk), 1)
    causal = col <= row
    tril = causal.astype(F32)
    sel_r = lax.broadcasted_iota(jnp.int32, (SUBLANES, LANES), 0)
    sel_c = lax.broadcasted_iota(jnp.int32, (SUBLANES, LANES), 1)
    lane0 = LANES // 2
    sel = ((sel_c == sel_r + lane0) & (sel_r < n_heads)).astype(F32)
    eye = (lax.broadcasted_iota(jnp.int32, (dk, dk), 0)
           == lax.broadcasted_iota(jnp.int32, (dk, dk), 1)).astype(BF16)
    hi = lax.Precision.HIGHEST

    for sq in range(n_sub):
        rs = slice(sq * L, (sq + 1) * L)
        pre = zs_ref[rs, :] + bias_ref[...]
        cap = GATE_CAP * jnp.tanh(pre / GATE_CAP)
        i_all = cap
        f_all = jnp.minimum(cap, 0.0) - jnp.log1p(jnp.exp(-jnp.abs(cap)))
        bcum_all = jnp.dot(tril, pad_rows(f_all), preferred_element_type=F32, precision=hi)
        r_all = i_all - pltpu.roll(bcum_all, LANES - n_heads, 1)
        r_rows = _dot_nt(sel, pad_rows(r_all), precision=hi)
        for h in range(n_heads):
            i_col = i_all[:, lane0 + h: lane0 + h + 1]
            b_col = bcum_all[:, lane0 + n_heads + h: lane0 + n_heads + h + 1]
            r_row = r_rows[h:h + 1, :]
            b_tot = b_col[L - 1:L, :]
            m_prev = m_ref[sq, :, h:h + 1]
            c_prev = c_ref[sq, h]
            n_prev = n_ref[sq, h:h + 1, :]

            q = q_ref[rs, h * dk:(h + 1) * dk] * (dk ** -0.5)
            k = k_ref[rs, h * dk:(h + 1) * dk]
            v = v_ref[rs, h * dv:(h + 1) * dv]
            qb = q.astype(BF16)
            kpb = pad_rows(k).astype(BF16)
            vpb = pad_rows(v).astype(BF16)

            dlog = jnp.where(causal, b_col + r_row, NEG_INF)
            inter = b_col + m_prev
            m_t = jnp.maximum(inter, jnp.max(dlog, axis=-1, keepdims=True))
            s = _dot_nt(qb, kpb) * jnp.exp(dlog - m_t)
            w_prev = jnp.exp(inter - m_t)
            num = _dot(s.astype(BF16), vpb) + w_prev * _dot(qb, c_prev.astype(BF16))
            den = (jnp.sum(s, axis=-1, keepdims=True)
                   + w_prev * jnp.sum(q * n_prev, axis=-1, keepdims=True))
            hout = num / jnp.maximum(jnp.abs(den), jnp.exp(-m_t))

            g = b_tot - b_col + i_col
            m_new = jnp.maximum(b_tot + m_prev, jnp.max(g, axis=0, keepdims=True))
            wk = jnp.exp(g - m_new)
            decay = jnp.exp(b_tot + m_prev - m_new)
            kw = wk * k
            kw_t = _dot_nt(eye, pad_rows(kw).astype(BF16)).astype(BF16)
            c_ref[sq, h] = decay * c_prev + _dot(kw_t, vpb)
            n_ref[sq, h:h + 1, :] = decay * n_prev + jnp.sum(kw, axis=0, keepdims=True)
            m_ref[sq, :, h:h + 1] = m_new

            hn = _rms(hout, norm_ref[:, h * dv:(h + 1) * dv])
            hn = hn * jax.nn.sigmoid(mo_ref[rs, h * dv:(h + 1) * dv])
            hm_ref[rs, h * dv:(h + 1) * dv] = hn.astype(hm_ref.dtype)


def _mlstm(z, zs, bias, norm, n_seq, seq, L, n_sub, rows0, n_heads, dk, dv, m_total, state, prev_hm):
    wq = n_heads * dk
    wv = n_heads * dv
    n_chunks = seq // L
    rb = n_sub * L
    has_state = state is not None
    if has_state:
        assert n_chunks == 1
    ro = rows0 // rb

    def rmap(cb):
        return lambda b, c: (ro + b * n_chunks + c, cb)

    ins = [z, z, z, z, zs, bias, norm]
    in_specs = [
        pl.BlockSpec((rb, wq), rmap(0)),
        pl.BlockSpec((rb, wq), rmap(1)),
        pl.BlockSpec((rb, wv), rmap(1)),
        pl.BlockSpec((rb, wv), rmap(2)),
        pl.BlockSpec((rb, LANES), rmap(0)),
        pl.BlockSpec((1, LANES), lambda b, c: (0, 0)),
        pl.BlockSpec((1, wv), lambda b, c: (0, 0)),
    ]
    c_spec = pl.BlockSpec((n_sub, n_heads, dk, dv), lambda b, c: (b, 0, 0, 0))
    n_spec = pl.BlockSpec((n_sub, n_heads, dk), lambda b, c: (b, 0, 0))
    m_spec = pl.BlockSpec((n_sub, 1, n_heads), lambda b, c: (b, 0, 0))
    aliases = {}
    if has_state:
        ins += [state[0], state[1], state[2], prev_hm]
        in_specs += [c_spec, n_spec, m_spec, pl.BlockSpec(memory_space=pl.ANY)]
        aliases = {10: 0}
    return pl.pallas_call(
        functools.partial(_mlstm_kernel, n_sub, L, n_heads, dk, dv, has_state),
        grid=(n_seq // n_sub, n_chunks),
        in_specs=in_specs,
        out_specs=[pl.BlockSpec((rb, wv), rmap(0)), c_spec, n_spec, m_spec],
        out_shape=[SDS((m_total, wv), BF16), SDS((n_seq, n_heads, dk, dv), F32),
                   SDS((n_seq, n_heads, dk), F32), SDS((n_seq, 1, n_heads), F32)],
        input_output_aliases=aliases,
        compiler_params=_cparams(("arbitrary", "arbitrary")),
        name="mlstm_state" if has_state else "mlstm_prompt",
    )(*ins)


def _outproj_kernel(hm_ref, att_ref, w1_ref, w2_ref, x_ref, *rest):
    o_ref = rest[-1]
    o_ref[...] = x_ref[...] + _dot(hm_ref[...], w1_ref[...]) + _dot(att_ref[...], w2_ref[...])


def _out_proj(hm, att, w, x, rows0, m_total, prev):
    rows, d = x.shape
    k1, k2 = hm.shape[1], att.shape[1]
    assert k1 == k2 and w.shape[0] == k1 + k2
    tm = _pick(math.gcd(rows, rows0) if rows0 else rows, 1024, 16)
    tn = _pick(d, 512, LANES)
    ro = rows0 // tm
    ins = [hm, att, w, w, x]
    in_specs = [
        pl.BlockSpec((tm, k1), lambda i, j: (i + ro, 0)),
        pl.BlockSpec((tm, k2), lambda i, j: (i + ro, 0)),
        pl.BlockSpec((k1, tn), lambda i, j: (0, j)),
        pl.BlockSpec((k2, tn), lambda i, j: (1, j)),
        pl.BlockSpec((tm, tn), lambda i, j: (i, j)),
    ]
    aliases = {}
    if prev is not None:
        ins.append(prev)
        in_specs.append(pl.BlockSpec(memory_space=pl.ANY))
        aliases = {5: 0}
    return pl.pallas_call(
        _outproj_kernel,
        grid=(rows // tm, d // tn),
        in_specs=in_specs,
        out_specs=pl.BlockSpec((tm, tn), lambda i, j: (i + ro, j)),
        out_shape=SDS((m_total, d), F32),
        input_output_aliases=aliases,
        compiler_params=_cparams(("arbitrary", "arbitrary")),
        name="out_proj",
    )(*ins)


MOE_BLOCK = 512
MOE_SUB = 256
MOE_TILE = 256
WAIT_GROUP = 8
MOE_K_CHUNK = 1024


def _row_pitch(d):
    return d // LANES + SUBLANES


def _store_token_rows(ref2d, val, row0, pitch):
    tm = val.shape[0]
    for cc in range(val.shape[1] // LANES):
        ref2d[pl.ds(row0 + cc, tm, stride=pitch), :] = val[:, cc * LANES:(cc + 1) * LANES]


def _router_kernel(pitch, h_ref, g_ref, w_ref, b_ref, xp_ref, lg_ref):
    xn = _rms(h_ref[...], g_ref[...])
    _store_token_rows(xp_ref, xn, 0, pitch)
    lg_ref[...] = jnp.dot(xn, w_ref[...], preferred_element_type=F32,
                          precision=lax.Precision.HIGHEST) + b_ref[...]


def _router(h, g, w_rg, b_rg):
    m, d = h.shape
    tm = _pick(m, 256, 16)
    pitch = _row_pitch(d)
    return pl.pallas_call(
        functools.partial(_router_kernel, pitch),
        grid=(m // tm,),
        in_specs=[
            pl.BlockSpec((tm, d), lambda i: (i, 0)),
            pl.BlockSpec((1, d), lambda i: (0, 0)),
            pl.BlockSpec((d, LANES), lambda i: (0, 0)),
            pl.BlockSpec((1, LANES), lambda i: (0, 0)),
        ],
        out_specs=[pl.BlockSpec((tm * pitch, LANES), lambda i: (i, 0)),
                   pl.BlockSpec((tm, LANES), lambda i: (i, 0))],
        out_shape=[SDS((m * pitch, LANES), F32), SDS((m, LANES), F32)],
        compiler_params=_cparams(("arbitrary",)),
        name="ffn_router",
    )(h, g, w_rg, b_rg)


def _dispatch_kernel(tile, pitch, n_chunks, nu_ref, nv_ref, idx_ref, xp_hbm, o_ref, stage, sem):
    i = pl.program_id(0)
    nu = nu_ref[0]
    slot_rows = tile * pitch

    def copy(t, r, slot):
        src = pl.multiple_of(idx_ref[t * tile + r] * pitch, SUBLANES)
        dst = pl.multiple_of(slot * slot_rows + r * pitch, SUBLANES)
        return pltpu.make_async_copy(
            xp_hbm.at[pl.ds(src, n_chunks)], stage.at[pl.ds(dst, n_chunks)], sem.at[slot])

    def start_tile(t, slot):
        nv = nv_ref[t]

        def start(r2, c):
            copy(t, 2 * r2, slot).start(priority=0)
            copy(t, 2 * r2 + 1, slot).start(priority=1)
            return c
        lax.fori_loop(0, nv // 2, start, 0)

        @pl.when(nv % 2 == 1)
        def _():
            copy(t, nv - 1, slot).start(priority=0)

    @pl.when(i == 0)
    def _():
        stage[...] = jnp.zeros(stage.shape, F32)
        start_tile(0, 0)

    @pl.when(i < nu)
    def _():
        slot = i % 2

        @pl.when(i + 1 < nu)
        def _():
            start_tile(i + 1, 1 - slot)

        nv = nv_ref[i]
        group = pltpu.make_async_copy(
            xp_hbm.at[pl.ds(0, WAIT_GROUP * n_chunks)],
            stage.at[pl.ds(pl.multiple_of(slot * slot_rows, SUBLANES), WAIT_GROUP * n_chunks)], sem.at[slot])

        def wait_group(g, c):
            group.wait()
            return c
        lax.fori_loop(0, nv // WAIT_GROUP, wait_group, 0)

        def wait(r, c):
            copy(i, r, slot).wait()
            return c
        lax.fori_loop(0, nv % WAIT_GROUP, wait, 0)
        for c in range(n_chunks):
            v = stage[pl.ds(slot * slot_rows + c, tile, stride=pitch), :]
            o_ref[:, c * LANES:(c + 1) * LANES] = v.astype(BF16)


def _dispatch(xp, idx, n_used_tiles, n_valid, tile, d):
    n = idx.shape[0]
    nch = d // LANES
    pitch = _row_pitch(d)

    def blk(i, nu, nv, ix):
        return (jnp.minimum(i, nu[0] - 1), 0)

    return pl.pallas_call(
        functools.partial(_dispatch_kernel, tile, pitch, nch),
        grid_spec=pltpu.PrefetchScalarGridSpec(
            num_scalar_prefetch=3,
            grid=(n // tile,),
            in_specs=[pl.BlockSpec(memory_space=pl.ANY)],
            out_specs=pl.BlockSpec((tile, d), blk),
            scratch_shapes=[pltpu.VMEM((2 * tile * pitch, LANES), F32), pltpu.SemaphoreType.DMA((2,))],
        ),
        out_shape=SDS((n, d), BF16),
        compiler_params=_cparams(("arbitrary",)),
        name="moe_dispatch",
    )(n_used_tiles, n_valid, idx, xp)


def _weight_stream(be_ref, nu, parts, sem, b, f, nf):
    t = b * nf + f

    def copies(bb, ff, slot):
        e = be_ref[bb]
        return [pltpu.make_async_copy(src(e, ff), dst(slot), sem.at[slot, n])
                for n, (src, dst) in enumerate(parts)]

    def start(bb, ff, slot):
        for n, cp in enumerate(copies(bb, ff, slot)):
            cp.start(priority=n % 2)

    @pl.when(t == 0)
    def _():
        start(0, 0, 0)

    slot = t % 2
    t_next = t + 1
    b_next = t_next // nf

    @pl.when(b_next < nu)
    def _():
        start(b_next, t_next % nf, 1 - slot)

    for cp in copies(b, f, slot):
        cp.wait()
    return slot


def _moe_up_kernel(sub, nk, be_ref, nu_ref, nvb_ref, x_ref, wg_hbm, wu_hbm, hid_ref, wbuf, acc_ref, sem):
    b = pl.program_id(0)
    kc = pl.program_id(1)
    nu = nu_ref[0]
    tk = x_ref.shape[-1]
    de = hid_ref.shape[-1]

    def row_chunk(w_hbm):
        return lambda e, kk: w_hbm.at[e, pl.ds(pl.multiple_of(kk * tk, tk), tk), :]

    @pl.when(b < nu)
    def _():
        slot = _weight_stream(
            be_ref, nu,
            [(row_chunk(wg_hbm), lambda s: wbuf.at[s, 0]), (row_chunk(wu_hbm), lambda s: wbuf.at[s, 1])],
            sem, b, kc, nk)
        w_gu = jnp.concatenate([wbuf[slot, 0].astype(BF16), wbuf[slot, 1].astype(BF16)], axis=1)

        def sub_block(s):
            rs = slice(s * sub, (s + 1) * sub)
            part = _dot(x_ref[rs, :], w_gu)

            def finish(gu):
                hid_ref[rs, :] = (jax.nn.silu(gu[:, 0:de]) * gu[:, de:2 * de]).astype(BF16)

            if nk == 1:
                finish(part)
                return

            @pl.when(kc == 0)
            def _():
                acc_ref[rs, :] = part

            @pl.when((kc > 0) & (kc < nk - 1))
            def _():
                acc_ref[rs, :] += part

            @pl.when(kc == nk - 1)
            def _():
                finish(acc_ref[rs, :] + part)

        sub_block(0)
        for s in range(1, x_ref.shape[0] // sub):
            pl.when(nvb_ref[b] > s * sub)(functools.partial(sub_block, s))


def _moe_down_kernel(sub, pitch, nj, be_ref, nu_ref, nvb_ref, hid_ref, wd_hbm, yp_ref, wbuf, sem):
    b = pl.program_id(0)
    j = pl.program_id(1)
    nu = nu_ref[0]
    _, de, tn = wbuf.shape
    half = de // 2

    def row_half(h):
        return (lambda e, jj: wd_hbm.at[e, pl.ds(h * half, half), pl.ds(pl.multiple_of(jj * tn, tn), tn)],
                lambda s: wbuf.at[s, pl.ds(h * half, half)])

    @pl.when(b < nu)
    def _():
        slot = _weight_stream(be_ref, nu, [row_half(0), row_half(1)], sem, b, j, nj)
        w_d = wbuf[slot].astype(BF16)

        def sub_block(s):
            y = _dot(hid_ref[s * sub:(s + 1) * sub, :], w_d)
            _store_token_rows(yp_ref, y, s * sub * pitch + j * (tn // LANES), pitch)

        sub_block(0)
        for s in range(1, hid_ref.shape[0] // sub):
            pl.when(nvb_ref[b] > s * sub)(functools.partial(sub_block, s))


def _moe_ffn(x_sorted, blk_expert, n_used, n_valid_blk, w_gate, w_up, w_down, tm):
    a_pad, d = x_sorted.shape
    n_exp, _, de = w_gate.shape
    n_blocks = a_pad // tm
    sub = min(MOE_SUB, tm)

    def blk(b, nu):
        return jnp.minimum(b, nu[0] - 1)

    def frozen(b, f, nu, last):
        return jnp.where(b < nu[0], f, last)

    tk = _pick(d, MOE_K_CHUNK, LANES)
    nk = d // tk
    hid = pl.pallas_call(
        functools.partial(_moe_up_kernel, sub, nk),
        grid_spec=pltpu.PrefetchScalarGridSpec(
            num_scalar_prefetch=3,
            grid=(n_blocks, nk),
            in_specs=[
                pl.BlockSpec((tm, tk), lambda b, k, be, nu, nv: (blk(b, nu), frozen(b, k, nu, nk - 1))),
                pl.BlockSpec(memory_space=pl.ANY),
                pl.BlockSpec(memory_space=pl.ANY),
            ],
            out_specs=pl.BlockSpec((tm, de), lambda b, k, be, nu, nv: (blk(b, nu), 0)),
            scratch_shapes=[pltpu.VMEM((2, 2, tk, de), F32), pltpu.VMEM((tm, 2 * de), F32),
                            pltpu.SemaphoreType.DMA((2, 2))],
        ),
        out_shape=SDS((a_pad, de), BF16),
        compiler_params=_cparams(("arbitrary", "arbitrary")),
        name="moe_up",
    )(blk_expert, n_used, n_valid_blk, x_sorted, w_gate, w_up)

    tn = _pick(d, 1024, LANES)
    nj = d // tn
    pitch = _row_pitch(d)
    return pl.pallas_call(
        functools.partial(_moe_down_kernel, sub, pitch, nj),
        grid_spec=pltpu.PrefetchScalarGridSpec(
            num_scalar_prefetch=3,
            grid=(n_blocks, nj),
            in_specs=[
                pl.BlockSpec((tm, de), lambda b, j, be, nu, nv: (blk(b, nu), 0)),
                pl.BlockSpec(memory_space=pl.ANY),
            ],
            out_specs=pl.BlockSpec((tm * pitch, LANES), lambda b, j, be, nu, nv: (blk(b, nu), 0)),
            scratch_shapes=[pltpu.VMEM((2, de, tn), F32), pltpu.SemaphoreType.DMA((2, 2))],
        ),
        out_shape=SDS((a_pad * pitch, LANES), F32),
        compiler_params=_cparams(("arbitrary", "arbitrary")),
        name="moe_down",
    )(blk_expert, n_used, n_valid_blk, hid, w_down)


def _combine_kernel(tile, pitch, n_chunks, pos_ref, h_ref, wt_ref, yp_hbm, h2_ref, h2b_ref, ybuf, sem):
    i = pl.program_id(0)
    n_tiles = pl.num_programs(0)
    n_rows = TOP_K * tile
    slot_rows = n_rows * pitch

    def copy(t, r, slot):
        src = pl.multiple_of(pos_ref[t * n_rows + r] * pitch, SUBLANES)
        dst = pl.multiple_of(slot * slot_rows + r * pitch, SUBLANES)
        return pltpu.make_async_copy(
            yp_hbm.at[pl.ds(src, n_chunks)], ybuf.at[pl.ds(dst, n_chunks)], sem.at[slot])

    def start_tile(t, slot):
        def start(r2, c):
            copy(t, 2 * r2, slot).start(priority=0)
            copy(t, 2 * r2 + 1, slot).start(priority=1)
            return c
        lax.fori_loop(0, n_rows // 2, start, 0, unroll=4)

    @pl.when(i == 0)
    def _():
        start_tile(0, 0)

    slot = i % 2

    @pl.when(i + 1 < n_tiles)
    def _():
        start_tile(i + 1, 1 - slot)

    def wait(r, c):
        copy(i, r, slot).wait()
        return c
    lax.fori_loop(0, n_rows, wait, 0, unroll=8)
    wt = wt_ref[...]
    for c in range(n_chunks):
        cs = slice(c * LANES, (c + 1) * LANES)
        h2 = h_ref[:, cs]
        for kk in range(TOP_K):
            row0 = slot * slot_rows + kk * tile * pitch + c
            h2 = h2 + wt[:, kk:kk + 1] * ybuf[pl.ds(row0, tile, stride=pitch), :]
        h2_ref[:, cs] = h2
        h2b_ref[:, cs] = h2.astype(BF16)


def _combine(h, yp, pos, wts):
    m, d = h.shape
    tile = _pick(m, 256, 16)
    n_tiles = m // tile
    pitch = _row_pitch(d)
    pos_tiled = pos.reshape(n_tiles, tile, TOP_K).transpose(0, 2, 1).reshape(-1)
    return pl.pallas_call(
        functools.partial(_combine_kernel, tile, pitch, d // LANES),
        grid_spec=pltpu.PrefetchScalarGridSpec(
            num_scalar_prefetch=1,
            grid=(n_tiles,),
            in_specs=[
                pl.BlockSpec((tile, d), lambda i, p: (i, 0)),
                pl.BlockSpec((tile, TOP_K), lambda i, p: (i, 0)),
                pl.BlockSpec(memory_space=pl.ANY),
            ],
            out_specs=[pl.BlockSpec((tile, d), lambda i, p: (i, 0)),
                       pl.BlockSpec((tile, d), lambda i, p: (i, 0))],
            scratch_shapes=[pltpu.VMEM((2 * TOP_K * tile * pitch, LANES), F32),
                            pltpu.SemaphoreType.DMA((2,))],
        ),
        out_shape=[SDS((m, d), F32), SDS((m, d), BF16)],
        compiler_params=_cparams(("arbitrary",)),
        name="moe_combine",
    )(pos_tiled, h, wts, yp)


def _ple_kernel(final_norm, nj, hb_ref, h_ref, p_ref, wg_ref, wp_ref, g_ref, o_ref, acc_ref):
    j = pl.program_id(1)
    gate = jax.nn.sigmoid(_dot(hb_ref[...], wg_ref[...]))
    proj = _dot(p_ref[...].astype(BF16), wp_ref[...])
    acc_ref[j] = h_ref[...] + gate * proj

    @pl.when(j == nj - 1)
    def _():
        tn = acc_ref.shape[-1]
        if final_norm:
            ssq = jnp.zeros((acc_ref.shape[1], 1), F32)
            for c in range(nj):
                a = acc_ref[c]
                ssq = ssq + jnp.sum(a * a, axis=-1, keepdims=True)
            rs = lax.rsqrt(ssq / (nj * tn) + EPS)
            for c in range(nj):
                o_ref[:, c * tn:(c + 1) * tn] = acc_ref[c] * rs * g_ref[:, c * tn:(c + 1) * tn]
        else:
            for c in range(nj):
                o_ref[:, c * tn:(c + 1) * tn] = acc_ref[c]


def _ple(h2b, h2, p, w_gate, w_proj, g, rows0, rows, final_norm):
    d = h2.shape[1]
    pd = p.shape[1]
    tm = _pick(math.gcd(rows, rows0) if rows0 else rows, 512, 16)
    tn = _pick(d, 512, LANES)
    nj = d // tn
    ro = rows0 // tm
    return pl.pallas_call(
        functools.partial(_ple_kernel, final_norm, nj),
        grid=(rows // tm, nj),
        in_specs=[
            pl.BlockSpec((tm, d), lambda i, j: (i + ro, 0)),
            pl.BlockSpec((tm, tn), lambda i, j: (i + ro, j)),
            pl.BlockSpec((tm, pd), lambda i, j: (i, 0)),
            pl.BlockSpec((d, tn), lambda i, j: (0, j)),
            pl.BlockSpec((pd, tn), lambda i, j: (0, j)),
            pl.BlockSpec((1, d), lambda i, j: (0, 0)),
        ],
        out_specs=pl.BlockSpec((tm, d), lambda i, j: (i, 0)),
        out_shape=SDS((rows, d), F32),
        scratch_shapes=[pltpu.VMEM((nj, tm, tn), F32)],
        compiler_params=_cparams(("arbitrary", "arbitrary")),
        name="ple_gate",
    )(h2b, h2, p, w_gate, w_proj, g)


def _route(logits, n_groups, epg, tm, tile):
    m = logits.shape[0]
    n_exp = n_groups * epg
    g_logits = logits[:, :n_groups]
    g_idx = jnp.argmax(g_logits, axis=-1)
    g_w = 1.0 / jnp.sum(jnp.exp(g_logits - jnp.max(g_logits, axis=-1, keepdims=True)), axis=-1)
    e_logits = logits[:, n_groups:n_groups + n_exp].reshape(m, n_groups, epg)
    in_group = (jnp.arange(n_groups)[None, :] == g_idx[:, None])[:, :, None]
    e_in = jnp.sum(jnp.where(in_group, e_logits, 0.0), axis=1)
    lane = jnp.arange(epg)[None, :]
    first = jnp.argmax(e_in, axis=-1)
    v1 = jnp.max(e_in, axis=-1)
    rest = jnp.where(lane == first[:, None], NEG_INF, e_in)
    second = jnp.argmax(rest, axis=-1)
    v2 = jnp.max(rest, axis=-1)
    w2 = jnp.exp(v2 - v1)
    e_w = jnp.stack([1.0 / (1.0 + w2), w2 / (1.0 + w2)], axis=-1)
    ids = g_idx[:, None] * epg + jnp.stack([first, second], axis=-1)
    wts = g_w[:, None] * e_w

    a = m * TOP_K
    e_flat = ids.reshape(-1).astype(jnp.int32)
    onehot = (e_flat[:, None] == jnp.arange(n_exp, dtype=jnp.int32)[None, :]).astype(jnp.int32)
    counts = jnp.sum(onehot, axis=0)
    rank = jnp.sum((jnp.cumsum(onehot, axis=0) - onehot) * onehot, axis=1)
    padded = (counts + tm - 1) // tm * tm
    pad_end = jnp.cumsum(padded)
    pad_start = pad_end - padded
    dest = (pad_start[e_flat] + rank).astype(jnp.int32)
    n_blocks = -(-a // tm) + n_exp
    tok = jnp.arange(a, dtype=jnp.int32) // TOP_K
    row_src = jnp.zeros((n_blocks * tm,), jnp.int32).at[dest].set(tok, unique_indices=True)
    n_used = (pad_end[-1] // tm).astype(jnp.int32)
    blk_id = jnp.arange(n_blocks, dtype=jnp.int32)
    blk_expert = jnp.minimum(jnp.searchsorted(pad_end, blk_id * tm, side="right"), n_exp - 1)
    last = blk_expert[jnp.maximum(n_used - 1, 0)]
    blk_expert = jnp.where(blk_id < n_used, blk_expert, last).astype(jnp.int32)
    tile_id = jnp.arange(n_blocks * tm // tile, dtype=jnp.int32)
    tile_e = blk_expert[tile_id * tile // tm]
    valid_end = (pad_start + counts)[tile_e]
    n_valid = jnp.clip(valid_end - tile_id * tile, 0, tile)
    n_valid = jnp.where(tile_id * tile < pad_end[-1], n_valid, 0).astype(jnp.int32)
    n_used_tiles = (n_used * (tm // tile)).reshape(1)
    n_valid_blk = jnp.sum(n_valid.reshape(n_blocks, tm // tile), axis=1).astype(jnp.int32)
    return (row_src, dest.reshape(m, TOP_K), wts, blk_expert, n_used.reshape(1), n_used_tiles, n_valid,
            n_valid_blk)


def kernel(x_prompt, x_sample, p_prompt, p_sample, cache_ckv, cache_krope, state_C, state_n, state_m, page_table, norm_mix, w_in, b_igate, b_fgate, mlstm_norm, q_a_norm, w_q_b, kv_a_norm, w_kv_b, w_out, norm_ffn, w_group, b_group, w_router, b_router, w_gate, w_up, w_down, w_ple_proj, w_ple_gate, norm_final):
    depth = w_in.shape[0]
    bp, seq, d = x_prompt.shape
    n_dec, ts, _ = x_sample.shape
    mp, ms = bp * seq, n_dec * ts
    m = mp + ms
    hm_heads = b_igate.shape[-1]
    dk, dv = state_n.shape[-1], state_C.shape[-1]
    ql, kvl, rope = q_a_norm.shape[-1], kv_a_norm.shape[-1], cache_krope.shape[-1]
    att_w = d - hm_heads * dv
    a_heads = (w_q_b.shape[-1] - w_kv_b.shape[-1] + att_w) // rope
    av = att_w // a_heads
    n_groups, n_exp = w_group.shape[-1], w_router.shape[-1]
    epg = n_exp // n_groups
    n_pages, page = page_table.shape[1], cache_ckv.shape[2]
    past = n_pages * page
    wq, wv = hm_heads * dk, hm_heads * dv
    assert rope == LANES // 2 and a_heads % 2 == 0 and w_q_b.shape[-1] == a_heads * (A_NOPE + rope)
    assert dv == 2 * dk and mp % ms == 0 and n_groups + n_exp <= LANES and 2 * hm_heads <= LANES // 2

    half = rope // 2
    inv = 1.0 / (ROPE_THETA ** (jnp.arange(half, dtype=F32) / half))
    pos = jnp.concatenate([jnp.tile(jnp.arange(seq), bp), jnp.tile(past + jnp.arange(ts), n_dec)])
    ang = pos.astype(F32)[:, None] * inv[None, :]
    cos_t = jnp.tile(jnp.cos(ang), (1, LANES // half))
    sin_t = jnp.tile(jnp.concatenate([-jnp.sin(ang), jnp.sin(ang)], axis=1), (1, LANES // rope))

    hp = x_prompt.reshape(mp, d)
    hs = x_sample.reshape(ms, d)
    outs = {k: [] for k in ("ckv_p", "kr_p", "C_p", "n_p", "m_p", "ckv_s", "kr_s", "C_s", "n_s", "m_s")}
    for l in range(depth):
        wl = w_in[l]
        o_aq = 2 * wq + 2 * wv + 2 * hm_heads
        o_c = o_aq + ql
        o_kr = o_c + kvl
        w_main = jnp.concatenate([wl[:, :2 * wq + 2 * wv], wl[:, o_aq:o_kr]], axis=1).astype(BF16)
        n_small = rope + 2 * hm_heads
        w_small = jnp.concatenate(
            [wl[:, o_kr:o_kr + rope], wl[:, 2 * wq + 2 * wv:o_aq], jnp.zeros((d, LANES - n_small), F32)],
            axis=1).astype(BF16)
        gate_bias = jnp.concatenate(
            [jnp.zeros((rope,), F32), b_igate[l], b_fgate[l], jnp.zeros((LANES - n_small,), F32)]).reshape(1, LANES)
        wqb = w_q_b[l].reshape(ql, a_heads, A_NOPE + rope)
        wq_re = jnp.concatenate(
            [wqb[:, :, :A_NOPE].reshape(ql, -1), wqb[:, :, A_NOPE:].reshape(ql, -1)], axis=1).astype(BF16)
        wkvb = w_kv_b[l].astype(BF16)
        wkv3 = w_kv_b[l].reshape(kvl, a_heads, A_NOPE + av)
        w_ukt = wkv3[:, :, :A_NOPE].transpose(1, 2, 0).astype(BF16)
        w_uv = wkv3[:, :, A_NOPE:].transpose(1, 0, 2).astype(BF16)
        w_o = w_out[l].astype(BF16)
        w_rg = jnp.concatenate(
            [w_group[l], w_router[l], jnp.zeros((d, LANES - n_groups - n_exp), F32)], axis=1)
        b_rg = jnp.concatenate(
            [b_group[l], b_router[l], jnp.zeros((LANES - n_groups - n_exp,), F32)]).reshape(1, LANES)
        w_pg = w_ple_gate[l].astype(BF16)
        w_pp = w_ple_proj[l].astype(BF16)
        g_mix = norm_mix[l].reshape(1, d)

        zz = _in_proj(hp, g_mix, w_main, w_small, m, 0, None)
        z, zs = _in_proj(hs, g_mix, w_main, w_small, m, mp, zz)
        q_hm, ckv, kr = _mla_q(z, zs, cos_t, sin_t, q_a_norm[l].reshape(1, ql), kv_a_norm[l].reshape(1, kvl),
                               wq_re, a_heads, ql, kvl, rope, 2 * wq + 2 * wv, 2 * wq + 2 * wv + ql)
        ckv_p, ckv_s, kr_p, kr_s = ckv[:mp], ckv[mp:], kr[:mp], kr[mp:]

        norm_m = mlstm_norm[l].reshape(1, wv)
        lp = _pick(seq, 256, 16)
        hm, c_p, n_p, m_p = _mlstm(z, zs, gate_bias, norm_m, bp, seq, lp, 1, 0, hm_heads, dk, dv, m, None, None)
        n_sub = 2 if n_dec % 2 == 0 else 1
        hm, c_s, n_s, m_s = _mlstm(z, zs, gate_bias, norm_m, n_dec, ts, ts, n_sub, mp, hm_heads, dk, dv, m,
                                   (state_C[l], state_n[l], state_m[l].reshape(n_dec, 1, hm_heads)), hm)

        k_hm, v_hm = _kv_expand(ckv_p, kr_p, wkvb, mp, a_heads, av)
        att = _flash_prompt(q_hm, k_hm, v_hm, bp, seq, m)
        qs = _q_latent(q_hm, w_ukt, mp, n_dec, ts)
        o_s = _mla_sample(page_table, qs, ckv_s, kr_s, cache_ckv[l], jnp.swapaxes(cache_krope[l], 1, 2))
        att = _o_uv(o_s, w_uv, att, mp)

        h1 = _out_proj(hm, att, w_o, hp, 0, m, None)
        h1 = _out_proj(hm, att, w_o, hs, mp, m, h1)

        xn2p, logits = _router(h1, norm_ffn[l].reshape(1, d), w_rg, b_rg)
        row_src, posn, wts, blk_expert, n_used, n_used_tiles, n_valid, n_valid_blk = _route(
            logits, n_groups, epg, MOE_BLOCK, MOE_TILE)
        x_sorted = _dispatch(xn2p, row_src, n_used_tiles, n_valid, MOE_TILE, d)
        yp = _moe_ffn(x_sorted, blk_expert, n_used, n_valid_blk, w_gate[l], w_up[l], w_down[l], MOE_BLOCK)
        h2, h2b = _combine(h1, yp, posn, wts)

        last = l == depth - 1
        g_fin = norm_final.reshape(1, d)
        hp = _ple(h2b, h2, p_prompt[l].reshape(mp, -1), w_pg, w_pp, g_fin, 0, mp, last)
        hs = _ple(h2b, h2, p_sample[l].reshape(ms, -1), w_pg, w_pp, g_fin, mp, ms, last)

        outs["ckv_p"].append(ckv_p.reshape(bp, seq, kvl))
        outs["kr_p"].append(kr_p.reshape(bp, seq, rope))
        outs["C_p"].append(c_p)
        outs["n_p"].append(n_p)
        outs["m_p"].append(m_p.reshape(bp, hm_heads))
        outs["ckv_s"].append(ckv_s.reshape(n_dec, ts, kvl))
        outs["kr_s"].append(kr_s.reshape(n_dec, ts, rope))
        outs["C_s"].append(c_s)
        outs["n_s"].append(n_s)
        outs["m_s"].append(m_s.reshape(n_dec, hm_heads))

    st = {k: jnp.stack(v) for k, v in outs.items()}
    return (hp.reshape(bp, seq, d), hs.reshape(n_dec, ts, d),
            st["ckv_p"], st["kr_p"], st["C_p"], st["n_p"], st["m_p"],
            st["ckv_s"], st["kr_s"], st["C_s"], st["n_s"], st["m_s"])
```
